```python
import math
import jax, jax.numpy as jnp
from jax import lax
import numpy as np

D_MODEL = 4096
BATCH = 2
SEQ = 4096
DEPTH = 2

GRID_W = 64
CTX_LEN = 256
ATTN_HEADS = 16
ATTN_KV_HEADS = 4
HEAD_DIM = 128
ATTN_GROUP = ATTN_HEADS // ATTN_KV_HEADS
ATTN_DIM = ATTN_HEADS * HEAD_DIM
KV_DIM = ATTN_KV_HEADS * HEAD_DIM
WINDOW = 128
BLOCK = 128
ROPE_THETA = 10000.0
SSM_HEADS = 32
SSM_HEAD_DIM = 64
SSM_DIM = SSM_HEADS * SSM_HEAD_DIM
SSM_GROUPS = 4
SSM_HEADS_PER_GROUP = SSM_HEADS // SSM_GROUPS
SSM_STATE = 128
BC_DIM = SSM_GROUPS * SSM_STATE
CONV_K = 5
CONV_DIM = SSM_DIM + 2 * BC_DIM
CHUNK = 128
N_DIR = 2
D_MIX = ATTN_DIM + SSM_DIM
IN_DIM = ATTN_DIM + SSM_DIM + 2 * KV_DIM + CONV_DIM + N_DIR * SSM_HEADS
CTX_COL0 = ATTN_DIM + SSM_DIM
D_FF = 11008
N_EXPERTS = 8
TOP_K = 2
D_EXPERT = 4096
N_DENSE = (DEPTH + 1) // 2
N_MOE = DEPTH // 2
N_MOD = 6
EPS = 1e-6

kernel_name = "hybrid_swa_ssd_moe_dit_trunk"


def rms_norm(x, g):
    xf = x.astype(jnp.float32)
    xf = xf * lax.rsqrt(jnp.mean(xf * xf, axis=-1, keepdims=True) + EPS)
    return (xf * g.astype(jnp.float32)).astype(x.dtype)


def axial_rope(u, row, col):
    half = HEAD_DIM // 2
    freqs = ROPE_THETA ** (-jnp.arange(0, half, 2, dtype=jnp.float32) / half)

    def rot(v, pos):
        ang = pos.astype(jnp.float32)[:, None] * freqs
        cos = jnp.cos(ang)[:, None, :].astype(v.dtype)
        sin = jnp.sin(ang)[:, None, :].astype(v.dtype)
        v1, v2 = jnp.split(v, 2, axis=-1)
        return jnp.concatenate([v1 * cos - v2 * sin, v2 * cos + v1 * sin], axis=-1)

    return jnp.concatenate([rot(u[..., :half], row), rot(u[..., half:], col)], axis=-1)


def softmax_with_sink(logits, sink):
    full = jnp.concatenate([logits, jnp.broadcast_to(sink, logits.shape[:-1] + (1,))], axis=-1)
    return jax.nn.softmax(full, axis=-1)[..., :-1]


def latent_window_attention(q, k, v, kc, vc, sink):
    b, s = q.shape[:2]
    nb = s // BLOCK
    scale = HEAD_DIM ** -0.5
    qb = q.reshape(b, nb, BLOCK, ATTN_KV_HEADS, ATTN_GROUP, HEAD_DIM)

    def band(u):
        up = jnp.pad(u, ((0, 0), (BLOCK, BLOCK), (0, 0), (0, 0)))
        up = up.reshape(b, nb + 2, BLOCK, ATTN_KV_HEADS, HEAD_DIM)
        return jnp.concatenate([up[:, :-2], up[:, 1:-1], up[:, 2:]], axis=2)

    kb, vb = band(k), band(v)
    s_loc = jnp.einsum('bnqhgd,bnkhd->bnhgqk', qb, kb).astype(jnp.float32) * scale
    blk = jnp.arange(nb)[:, None]
    qpos = blk * BLOCK + jnp.arange(BLOCK)[None, :]
    kpos = (blk - 1) * BLOCK + jnp.arange(3 * BLOCK)[None, :]
    valid = ((jnp.abs(qpos[:, :, None] - kpos[:, None, :]) <= WINDOW)
             & (kpos[:, None, :] >= 0) & (kpos[:, None, :] < s))
    s_loc = jnp.where(valid[None, :, None, None], s_loc, -jnp.inf)
    s_ctx = jnp.einsum('bnqhgd,blhd->bnhgql', qb, kc).astype(jnp.float32) * scale
    sink_b = sink.astype(jnp.float32).reshape(ATTN_KV_HEADS, ATTN_GROUP, 1, 1)
    p = softmax_with_sink(jnp.concatenate([s_loc, s_ctx], axis=-1), sink_b).astype(v.dtype)
    o = (jnp.einsum('bnhgqk,bnkhd->bnqhgd', p[..., :3 * BLOCK], vb)
         + jnp.einsum('bnhgql,blhd->bnqhgd', p[..., 3 * BLOCK:], vc))
    return o.reshape(b, s, ATTN_DIM)


def context_attention(qc, kc, vc, sink):
    b, l = qc.shape[:2]
    qg = qc.reshape(b, l, ATTN_KV_HEADS, ATTN_GROUP, HEAD_DIM)
    sc = jnp.einsum('bqhgd,bkhd->bhgqk', qg, kc).astype(jnp.float32) * HEAD_DIM ** -0.5
    p = softmax_with_sink(sc, sink.astype(jnp.float32).reshape(ATTN_KV_HEADS, ATTN_GROUP, 1, 1)).astype(vc.dtype)
    return jnp.einsum('bhgqk,bkhd->bqhgd', p, vc).reshape(b, l, ATTN_DIM)


def centred_dwconv(u, w, bias):
    out = lax.conv_general_dilated(
        u, w[:, None, :].astype(u.dtype), window_strides=(1,),
        padding=[(CONV_K // 2, CONV_K // 2)], dimension_numbers=('NWC', 'WIO', 'NWC'),
        feature_group_count=u.shape[-1])
    return jax.nn.silu(out + bias.astype(u.dtype))


def ssd_inputs(xbc, dt_raw, conv_w, conv_b, dt_bias):
    u = centred_dwconv(xbc, conv_w, conv_b).astype(jnp.float32)
    b, t = u.shape[:2]
    xs = u[..., :SSM_DIM].reshape(b, t, SSM_GROUPS, SSM_HEADS_PER_GROUP, SSM_HEAD_DIM)
    bm = u[..., SSM_DIM:SSM_DIM + BC_DIM].reshape(b, t, SSM_GROUPS, SSM_STATE)
    cm = u[..., SSM_DIM + BC_DIM:].reshape(b, t, SSM_GROUPS, SSM_STATE)
    dt = jax.nn.softplus(dt_raw.astype(jnp.float32).reshape(b, t, N_DIR, SSM_HEADS)
                         + dt_bias.astype(jnp.float32))
    return xs, bm, cm, dt.reshape(b, t, N_DIR, SSM_GROUPS, SSM_HEADS_PER_GROUP)


def ssd_chunked(x, dt, A, bm, cm, h0):
    b, t = x.shape[:2]
    nc = t // CHUNK
    a = (dt * A).reshape(b, nc, CHUNK, SSM_GROUPS, SSM_HEADS_PER_GROUP)
    xdt = (x * dt[..., None]).reshape(b, nc, CHUNK, SSM_GROUPS, SSM_HEADS_PER_GROUP, SSM_HEAD_DIM)
    bc = bm.reshape(b, nc, CHUNK, SSM_GROUPS, SSM_STATE)
    cc = cm.reshape(b, nc, CHUNK, SSM_GROUPS, SSM_STATE)
    a_cum = jnp.cumsum(a, axis=2)
    a_t = jnp.moveaxis(a_cum, 2, -1)
    seg = a_t[..., :, None] - a_t[..., None, :]
    causal = jnp.tril(jnp.ones((CHUNK, CHUNK), dtype=bool))
    decay_in = jnp.exp(jnp.where(causal, seg, -jnp.inf))
    cb = jnp.einsum('bclgn,bcsgn->bcgls', cc, bc)
    y_diag = jnp.einsum('bcgls,bcgels,bcsgep->bclgep', cb, decay_in, xdt)
    decay_to_end = jnp.exp(a_cum[:, :, -1:] - a_cum)
    states = jnp.einsum('bclgn,bclge,bclgep->bcgepn', bc, decay_to_end, xdt)
    chunk_decay = jnp.exp(a_cum[:, :, -1])

    def step(h, inp):
        dec, st = inp
        return dec[..., None, None] * h + st, h

    h_last, h_prev = lax.scan(step, h0, (jnp.moveaxis(chunk_decay, 1, 0), jnp.moveaxis(states, 1, 0)))
    h_prev = jnp.moveaxis(h_prev, 0, 1)
    y_off = jnp.einsum('bclgn,bcgepn,bclge->bclgep', cc, h_prev, jnp.exp(a_cum))
    return (y_diag + y_off).reshape(b, t, SSM_GROUPS, SSM_HEADS_PER_GROUP, SSM_HEAD_DIM), h_last


def ssd_final_state(x, dt, A, bm):
    a_cum = jnp.cumsum(dt * A, axis=1)
    decay_to_end = jnp.exp(a_cum[:, -1:] - a_cum)
    return jnp.einsum('btgn,btge,btgep->bgepn', bm, decay_to_end, x * dt[..., None])


def hybrid_mixer(h, hc, w_in, attn_sink, attn_norm, conv_w, conv_b, dt_bias, a_log, d_skip,
                 ssm_norm, w_out, last):
    b, s, _ = h.shape
    l = hc.shape[1]
    rows = s // GRID_W
    row = jnp.repeat(jnp.arange(rows, dtype=jnp.int32), GRID_W)
    col = jnp.tile(jnp.arange(GRID_W, dtype=jnp.int32), rows)
    off_kv = [KV_DIM, 2 * KV_DIM, 2 * KV_DIM + CONV_DIM]
    off_all = [ATTN_DIM, ATTN_DIM + SSM_DIM] + [CTX_COL0 + o for o in off_kv]

    q, z, k, v, xbc, dt_raw = jnp.split(h @ w_in, off_all, axis=-1)
    if last:
        kc, vc, xbcc, dtc_raw = jnp.split(hc @ w_in[:, CTX_COL0:], off_kv, axis=-1)
    else:
        qc, zc, kc, vc, xbcc, dtc_raw = jnp.split(hc @ w_in, off_all, axis=-1)
    kc = kc.reshape(b, l, ATTN_KV_HEADS, HEAD_DIM)
    vc = vc.reshape(b, l, ATTN_KV_HEADS, HEAD_DIM)

    q = axial_rope(q.reshape(b, s, ATTN_HEADS, HEAD_DIM), row, col)
    k = axial_rope(k.reshape(b, s, ATTN_KV_HEADS, HEAD_DIM), row, col)
    v = v.reshape(b, s, ATTN_KV_HEADS, HEAD_DIM)
    attn = rms_norm(latent_window_attention(q, k, v, kc, vc, attn_sink), attn_norm)

    A = -jnp.exp(a_log.astype(jnp.float32)).reshape(N_DIR, SSM_GROUPS, SSM_HEADS_PER_GROUP)
    d_vec = d_skip.astype(jnp.float32).reshape(SSM_GROUPS, SSM_HEADS_PER_GROUP, 1)
    xs, bm, cm, dt = ssd_inputs(xbc, dt_raw, conv_w, conv_b, dt_bias)
    xsc, bmc, cmc, dtc = ssd_inputs(xbcc, dtc_raw, conv_w, conv_b, dt_bias)
    flip = lambda u: jnp.flip(u, axis=1)

    def ssd_out(y, xs_, z_):
        y = (y + d_vec * xs_).reshape(z_.shape).astype(z_.dtype)
        return rms_norm(y * jax.nn.silu(z_), ssm_norm)

    if last:
        hc_f = ssd_final_state(xsc, dtc[:, :, 0], A[0], bmc)
        hc_b = ssd_final_state(flip(xsc), flip(dtc[:, :, 1]), A[1], flip(bmc))
    else:
        zero = jnp.zeros((b, SSM_GROUPS, SSM_HEADS_PER_GROUP, SSM_HEAD_DIM, SSM_STATE), jnp.float32)
        yc_f, hc_f = ssd_chunked(xsc, dtc[:, :, 0], A[0], bmc, cmc, zero)
        yc_b, hc_b = ssd_chunked(flip(xsc), flip(dtc[:, :, 1]), A[1], flip(bmc), flip(cmc), zero)
    y_f, _ = ssd_chunked(xs, dt[:, :, 0], A[0], bm, cm, hc_f)
    y_b, _ = ssd_chunked(flip(xs), flip(dt[:, :, 1]), A[1], flip(bm), flip(cm), hc_b)
    ssm = ssd_out(y_f + flip(y_b), xs, z)
    out = jnp.concatenate([attn, ssm], axis=-1) @ w_out
    if last:
        return out, None
    attn_c = rms_norm(context_attention(qc.reshape(b, l, ATTN_HEADS, HEAD_DIM), kc, vc, attn_sink), attn_norm)
    ssm_c = ssd_out(yc_f + flip(yc_b), xsc, zc)
    out_c = jnp.concatenate([attn_c, ssm_c], axis=-1) @ w_out
    return out, out_c


def swiglu(h, wg, wu, wd):
    return (jax.nn.silu(h @ wg) * (h @ wu)) @ wd


def moe_ffn(h, router, wg, wu, wd):
    tok = h.reshape(-1, D_MODEL)
    logits = (tok @ router).astype(jnp.float32)
    top_val, top_idx = lax.top_k(logits, TOP_K)
    gates = jax.nn.softmax(top_val, axis=-1)
    combine = jnp.sum(jax.nn.one_hot(top_idx, N_EXPERTS, dtype=jnp.float32) * gates[..., None], axis=1)
    out = jnp.zeros_like(tok)
    for e in range(N_EXPERTS):
        out = out + combine[:, e:e + 1].astype(tok.dtype) * swiglu(tok, wg[e], wu[e], wd[e])
    return out.reshape(h.shape)


def setup_inputs(seed: int = 0) -> dict:
    key = jax.random.key(seed)
    ks = jax.random.split(key, 32)
    f32 = jnp.float32
    nrm = lambda k, shape, scale: jax.random.normal(k, shape, f32) * scale
    gain = lambda k, shape: 1.0 + 0.1 * jax.random.normal(k, shape, f32)
    dt0 = jnp.exp(jax.random.uniform(ks[15], (DEPTH, N_DIR, SSM_HEADS), f32, math.log(1e-3), math.log(1e-1)))
    return {
        "x": nrm(ks[0], (BATCH, SEQ, D_MODEL), 1.0),
        "c": nrm(ks[1], (BATCH, D_MODEL), 1.0),
        "ctx": nrm(ks[2], (BATCH, CTX_LEN, D_MODEL), 1.0),
        "c_ctx": nrm(ks[3], (D_MODEL,), 1.0),
        "ada_w": nrm(ks[4], (DEPTH, D_MODEL, N_MOD * D_MODEL), 0.5 * D_MODEL ** -0.5),
        "ada_b": nrm(ks[5], (DEPTH, N_MOD * D_MODEL), 0.02),
        "norm_mix_pre": gain(ks[6], (DEPTH, D_MODEL)),
        "norm_mix_post": gain(ks[7], (DEPTH, D_MODEL)),
        "norm_ffn_pre": gain(ks[8], (DEPTH, D_MODEL)),
        "norm_ffn_post": gain(ks[9], (DEPTH, D_MODEL)),
        "w_in": nrm(ks[10], (DEPTH, D_MODEL, IN_DIM), D_MODEL ** -0.5),
        "attn_sink": nrm(ks[11], (DEPTH, ATTN_HEADS), 0.5),
        "attn_norm": gain(ks[12], (DEPTH, ATTN_DIM)),
        "conv_w": nrm(ks[13], (DEPTH, CONV_K, CONV_DIM), CONV_K ** -0.5),
        "conv_b": nrm(ks[14], (DEPTH, CONV_DIM), 0.02),
        "dt_bias": dt0 + jnp.log(-jnp.expm1(-dt0)),
        "a_log": jnp.log(jax.random.uniform(ks[16], (DEPTH, N_DIR, SSM_HEADS), f32, 1.0, 16.0)),
        "d_skip": gain(ks[17], (DEPTH, SSM_HEADS)),
        "ssm_norm": gain(ks[18], (DEPTH, SSM_DIM)),
        "w_out": nrm(ks[19], (DEPTH, D_MIX, D_MODEL), D_MIX ** -0.5),
        "ffn_w_gate": nrm(ks[20], (N_DENSE, D_MODEL, D_FF), D_MODEL ** -0.5),
        "ffn_w_up": nrm(ks[21], (N_DENSE, D_MODEL, D_FF), D_MODEL ** -0.5),
        "ffn_w_down": nrm(ks[22], (N_DENSE, D_FF, D_MODEL), D_FF ** -0.5),
        "moe_router": nrm(ks[23], (N_MOE, D_MODEL, N_EXPERTS), D_MODEL ** -0.5),
        "moe_w_gate": nrm(ks[24], (N_MOE, N_EXPERTS, D_MODEL, D_EXPERT), D_MODEL ** -0.5),
        "moe_w_up": nrm(ks[25], (N_MOE, N_EXPERTS, D_MODEL, D_EXPERT), D_MODEL ** -0.5),
        "moe_w_down": nrm(ks[26], (N_MOE, N_EXPERTS, D_EXPERT, D_MODEL), D_EXPERT ** -0.5),
    }


def reference(x, c, ctx, c_ctx, ada_w, ada_b, norm_mix_pre, norm_mix_post, norm_ffn_pre, norm_ffn_post,
              w_in, attn_sink, attn_norm, conv_w, conv_b, dt_bias, a_log, d_skip, ssm_norm, w_out,
              ffn_w_gate, ffn_w_up, ffn_w_down, moe_router, moe_w_gate, moe_w_up, moe_w_down):
    c_act = jax.nn.silu(c)
    cc_act = jax.nn.silu(c_ctx)
    xc = ctx
    l = ctx.shape[1]
    for i in range(DEPTH):
        last = i == DEPTH - 1
        mod = (c_act @ ada_w[i] + ada_b[i])[:, None, :]
        sh_m, sc_m, g_m, sh_f, sc_f, g_f = jnp.split(mod, N_MOD, axis=-1)
        n_ctx_mod = 2 if last else N_MOD
        modc = cc_act @ ada_w[i][:, :n_ctx_mod * D_MODEL] + ada_b[i][:n_ctx_mod * D_MODEL]
        mc = jnp.split(modc, n_ctx_mod)
        h = rms_norm(x, norm_mix_pre[i]) * (1 + sc_m) + sh_m
        hc = rms_norm(xc, norm_mix_pre[i]) * (1 + mc[1]) + mc[0]
        y, yc = hybrid_mixer(h, hc, w_in[i], attn_sink[i], attn_norm[i], conv_w[i], conv_b[i],
                             dt_bias[i], a_log[i], d_skip[i], ssm_norm[i], w_out[i], last)
        x = x + g_m * rms_norm(y, norm_mix_post[i])
        hf = rms_norm(x, norm_ffn_pre[i]) * (1 + sc_f) + sh_f
        if not last:
            xc = xc + mc[2] * rms_norm(yc, norm_mix_post[i])
            hfc = rms_norm(xc, norm_ffn_pre[i]) * (1 + mc[4]) + mc[3]
            hf = jnp.concatenate([hfc, hf], axis=1)
        if i % 2 == 0:
            j = i // 2
            f = swiglu(hf, ffn_w_gate[j], ffn_w_up[j], ffn_w_down[j])
        else:
            j = i // 2
            f = moe_ffn(hf, moe_router[j], moe_w_gate[j], moe_w_up[j], moe_w_down[j])
        if last:
            x = x + g_f * rms_norm(f, norm_ffn_post[i])
        else:
            x = x + g_f * rms_norm(f[:, l:], norm_ffn_post[i])
            xc = xc + mc[5] * rms_norm(f[:, :l], norm_ffn_post[i])
    return x
```

```python
import functools
import math
from typing import NamedTuple

import jax
import jax.numpy as jnp
from jax import lax
from jax.experimental import pallas as pl
from jax.experimental.pallas import tpu as pltpu

F32 = jnp.float32
BF16 = jnp.bfloat16

HEAD_DIM = 128
WINDOW = 128
BLOCK = 128
ROPE_THETA = 10000.0
SSM_HEAD_DIM = 64
SSM_STATE = 128
CONV_K = 5
CHUNK = 128
N_DIR = 2
GRID_W = 64
ATTN_KV_HEADS = 4
N_MOD = 6
EPS = 1e-6
MOD_ROWS = 8

V7X_VMEM_BYTES = 64 * 1024 * 1024
VMEM_LIMIT_CAP = 56 * 1024 * 1024
LANES = 128
BF16_SUBLANES = 16


class Dims(NamedTuple):
    B: int
    S: int
    L: int
    D: int
    AH: int
    KVH: int
    SH: int
    SG: int
    DFF: int
    E: int
    DE: int
    GW: int

    @property
    def AD(self):
        return self.AH * HEAD_DIM

    @property
    def KD(self):
        return self.KVH * HEAD_DIM

    @property
    def SD(self):
        return self.SH * SSM_HEAD_DIM

    @property
    def BC(self):
        return self.SG * SSM_STATE

    @property
    def CD(self):
        return self.SD + 2 * self.BC

    @property
    def K0(self):
        return self.AD + self.SD

    @property
    def V0(self):
        return self.K0 + self.KD

    @property
    def X0(self):
        return self.V0 + self.KD

    @property
    def DT0(self):
        return self.X0 + self.CD

    @property
    def NL(self):
        return self.B * self.S

    @property
    def NC(self):
        return self.B * self.L

    @property
    def M(self):
        return self.NL + self.NC


def _pick(n, prefs):
    for p in prefs:
        if n % p == 0:
            return p
    raise ValueError(f"no tile in {prefs} divides {n}")


def _params(sem, est_bytes):
    limit = int(min(max(est_bytes * 5 // 4 + (2 << 20), 16 << 20), VMEM_LIMIT_CAP))
    return pltpu.CompilerParams(dimension_semantics=sem, vmem_limit_bytes=limit)


def _silu(v):
    return v / (1.0 + jnp.exp(-v))


def _rms(v, g):
    return v * lax.rsqrt(jnp.mean(v * v, axis=-1, keepdims=True) + EPS) * g


def _mod_kernel(c_ref, w_ref, b_ref, o_ref):
    c = _silu(c_ref[...]).astype(BF16)
    o_ref[...] = jnp.dot(c, w_ref[...].astype(BF16), preferred_element_type=F32) + b_ref[...]


def _mod_call(cvec, ada_w, ada_b):
    depth, d, n = ada_w.shape
    tn = _pick(n, (512, 256, 128))
    est = 2 * d * tn * 4 + d * tn * 2 + 4 * MOD_ROWS * tn * 4
    return pl.pallas_call(
        _mod_kernel,
        grid=(depth, n // tn),
        in_specs=[
            pl.BlockSpec((MOD_ROWS, d), lambda l, j: (0, 0)),
            pl.BlockSpec((None, d, tn), lambda l, j: (l, 0, j)),
            pl.BlockSpec((None, 1, tn), lambda l, j: (l, 0, j)),
        ],
        out_specs=pl.BlockSpec((None, MOD_ROWS, tn), lambda l, j: (l, 0, j)),
        out_shape=jax.ShapeDtypeStruct((depth, MOD_ROWS, n), F32),
        compiler_params=_params(("arbitrary", "arbitrary"), est),
        name="adaln_mod",
    )(cvec, ada_w, ada_b.reshape(depth, 1, n))


def _norm_kernel(*refs, split, has_res, n_lat_blocks):
    refs = list(refs)
    x_ref = refs.pop(0)
    c_ref = refs.pop(0) if split else None
    if has_res:
        y_ref, gpost_ref, gate_ref = refs.pop(0), refs.pop(0), refs.pop(0)
    gpre_ref, sh_ref, sc_ref = refs.pop(0), refs.pop(0), refs.pop(0)
    if has_res:
        xo_ref = refs.pop(0)
    h_ref = refs.pop(0)

    def run(src_ref):
        x = src_ref[...]
        if has_res:
            x = x + gate_ref[...] * _rms(y_ref[...].astype(F32), gpost_ref[...])
            xo_ref[...] = x
        h_ref[...] = (_rms(x, gpre_ref[...]) * (1.0 + sc_ref[...]) + sh_ref[...]).astype(h_ref.dtype)

    if split:
        i = pl.program_id(0)
        pl.when(i < n_lat_blocks)(lambda: run(x_ref))
        pl.when(i >= n_lat_blocks)(lambda: run(c_ref))
    else:
        run(x_ref)


def _norm_call(dm, rows, x, ctx2d, y, g_post, gate_lk, g_pre, sh_lk, sc_lk, mod3, h_dtype):
    d = dm.D
    tm = _pick(dm.L, (256, 128))
    nblk = rows // tm
    nlat = dm.NL // tm
    split = ctx2d is not None
    has_res = y is not None

    def mod_spec(lk):
        layer, k = lk
        return pl.BlockSpec(
            (None, 1, d),
            lambda i: ((layer * MOD_ROWS + jnp.minimum((i * tm) // dm.S, dm.B)) * N_MOD + k, 0, 0))

    row_spec = pl.BlockSpec((tm, d), lambda i: (i, 0))
    vec_spec = pl.BlockSpec((1, d), lambda i: (0, 0))
    in_specs, args = [], []
    if split:
        in_specs += [pl.BlockSpec((tm, d), lambda i: (jnp.minimum(i, nlat - 1), 0)),
                     pl.BlockSpec((tm, d), lambda i: (jnp.maximum(i - nlat, 0), 0))]
        args += [x, ctx2d]
    else:
        in_specs += [row_spec]
        args += [x]
    if has_res:
        in_specs += [row_spec, vec_spec, mod_spec(gate_lk)]
        args += [y, g_post.reshape(1, d), mod3]
    in_specs += [vec_spec, mod_spec(sh_lk), mod_spec(sc_lk)]
    args += [g_pre.reshape(1, d), mod3, mod3]
    out_specs, out_shape = [], []
    if has_res:
        out_specs.append(row_spec)
        out_shape.append(jax.ShapeDtypeStruct((rows, d), F32))
    out_specs.append(row_spec)
    out_shape.append(jax.ShapeDtypeStruct((rows, d), h_dtype))
    est = 2 * tm * d * (4 + 4 + 2 + 4 + 4)
    outs = pl.pallas_call(
        functools.partial(_norm_kernel, split=split, has_res=has_res, n_lat_blocks=nlat),
        grid=(nblk,),
        in_specs=in_specs,
        out_specs=out_specs,
        out_shape=out_shape,
        compiler_params=_params(("arbitrary",), est),
        name="residual_prenorm" if has_res else "prenorm",
    )(*args)
    return outs if has_res else (None, outs[0])


def _rope(acc, cos, sin_signed, tn):
    lane = lax.broadcasted_iota(jnp.int32, acc.shape, 1)
    first = (lane % 64) < 32
    partner = jnp.where(first, pltpu.roll(acc, tn - 32, 1), pltpu.roll(acc, 32, 1))
    reps = tn // HEAD_DIM
    cos_t = jnp.concatenate([cos] * reps, axis=1) if reps > 1 else cos
    sin_t = jnp.concatenate([sin_signed] * reps, axis=1) if reps > 1 else sin_signed
    return acc * cos_t + partner * sin_t


def _mm_kernel(*refs, n_lhs, k_sizes, n_w, mode, rope_panels, grouped, tn):
    refs = list(refs)
    if grouped:
        te_ref, nt_ref = refs.pop(0), refs.pop(0)
    lhs_refs = [refs.pop(0) for _ in range(n_lhs)]
    w_refs = [refs.pop(0) for _ in range(n_w)]
    if rope_panels:
        cos_ref, sin_ref = refs.pop(0), refs.pop(0)
    o_ref = refs.pop(0)
    wb_refs = [refs.pop(0) for _ in range(n_w)]
    j = pl.program_id(0)
    m = pl.program_id(1)

    if grouped:
        nt = nt_ref[0]
        mc = jnp.minimum(m, nt - 1)
        prev = jnp.maximum(mc - 1, 0)
        new_panel = jnp.logical_or(m == 0, te_ref[mc] != te_ref[prev])
        active = m < nt
    else:
        new_panel = m == 0
        active = None

    @pl.when(new_panel)
    def _():
        for w_ref, wb_ref in zip(w_refs, wb_refs):
            wb_ref[...] = w_ref[...].astype(BF16)

    def compute():
        accs = []
        for wb_ref in wb_refs:
            acc, k0 = None, 0
            for l_ref, ks in zip(lhs_refs, k_sizes):
                part = jnp.dot(l_ref[...], wb_ref[k0:k0 + ks, :], preferred_element_type=F32)
                acc = part if acc is None else acc + part
                k0 += ks
            accs.append(acc)
        res = _silu(accs[0]) * accs[1] if mode == "swiglu" else accs[0]
        if rope_panels:
            is_rope = functools.reduce(jnp.logical_or, [j == p for p in rope_panels])

            @pl.when(is_rope)
            def _():
                o_ref[...] = _rope(res, cos_ref[...], sin_ref[...], tn).astype(o_ref.dtype)

            @pl.when(jnp.logical_not(is_rope))
            def _():
                o_ref[...] = res.astype(o_ref.dtype)
        else:
            o_ref[...] = res.astype(o_ref.dtype)

    if grouped:
        pl.when(active)(compute)

        @pl.when(jnp.logical_not(active))
        def _():
            o_ref[...] = jnp.zeros_like(o_ref)
    else:
        compute()


def _mm_call(lhs_list, w_list, w_group, *, rows, tm, tn, n_cols, out_dtype, mode="plain",
             rope=None, grouped=None, name="panel_matmul"):
    k_sizes = tuple(a.shape[1] for a in lhs_list)
    k_tot = sum(k_sizes)
    n_w = len(w_list)
    assert all(w.shape[1] == k_tot for w in w_list)
    n_panels = n_cols // tn
    n_blocks = rows // tm
    rope_panels = ()
    args, in_specs = [], []
    if grouped is not None:
        tile_expert, n_tiles = grouped
        row_idx = lambda j, m, te, nt: jnp.minimum(m, nt[0] - 1)
        lhs_map = lambda j, m, te, nt: (row_idx(j, m, te, nt), 0)
        w_map = lambda j, m, te, nt: (w_group + te[row_idx(j, m, te, nt)], 0, j)
        out_map = lambda j, m, te, nt: (m, j)
    else:
        lhs_map = lambda j, m: (m, 0)
        w_map = lambda j, m: (w_group, 0, j)
        out_map = lambda j, m: (m, j)
    for a, ks in zip(lhs_list, k_sizes):
        in_specs.append(pl.BlockSpec((tm, ks), lhs_map))
        args.append(a)
    for w in w_list:
        in_specs.append(pl.BlockSpec((None, k_tot, tn), w_map))
        args.append(w)
    if rope is not None:
        cos, sin_signed, rope_panels = rope
        in_specs += [pl.BlockSpec((tm, HEAD_DIM), lambda j, m: (m, 0))] * 2
        args += [cos, sin_signed]
    out_bytes = jnp.dtype(out_dtype).itemsize
    est = (n_w * (2 * k_tot * tn * 4 + k_tot * tn * 2) + 2 * tm * k_tot * 2 + 2 * tm * tn * out_bytes
           + (2 + n_w) * tm * tn * 4)
    kern = functools.partial(_mm_kernel, n_lhs=len(lhs_list), k_sizes=k_sizes, n_w=n_w, mode=mode,
                             rope_panels=tuple(rope_panels), grouped=grouped is not None, tn=tn)
    scratch = [pltpu.VMEM((k_tot, tn), BF16) for _ in range(n_w)]
    out_shape = jax.ShapeDtypeStruct((rows, n_cols), out_dtype)
    cp = _params(("arbitrary", "arbitrary"), est)
    if grouped is not None:
        grid_spec = pltpu.PrefetchScalarGridSpec(
            num_scalar_prefetch=2, grid=(n_panels, n_blocks), in_specs=in_specs,
            out_specs=pl.BlockSpec((tm, tn), out_map), scratch_shapes=scratch)
        return pl.pallas_call(kern, grid_spec=grid_spec, out_shape=out_shape, compiler_params=cp,
                              name=name)(tile_expert, n_tiles, *args)
    return pl.pallas_call(kern, grid=(n_panels, n_blocks), in_specs=in_specs,
                          out_specs=pl.BlockSpec((tm, tn), out_map), out_shape=out_shape,
                          scratch_shapes=scratch, compiler_params=cp, name=name)(*args)


def _conv_kernel(prev_ref, cur_ref, next_ref, w_ref, b_ref, o_ref, *, tb, seg_lat, seg_ctx, n_lat_rows):
    i = pl.program_id(0)
    row0 = i * tb
    in_lat = row0 < n_lat_rows
    off = jnp.where(in_lat, row0 % seg_lat, (row0 - n_lat_rows) % seg_ctx)
    seg = jnp.where(in_lat, seg_lat, seg_ctx)
    keep_prev = jnp.where(off != 0, 1.0, 0.0)
    keep_next = jnp.where(off + tb != seg, 1.0, 0.0)
    halo = BF16_SUBLANES
    prev = prev_ref[...].astype(F32)[halo - 8:, :] * keep_prev
    nxt = next_ref[...].astype(F32)[:8, :] * keep_next
    ext = jnp.concatenate([prev, cur_ref[...].astype(F32), nxt], axis=0)
    rows = tb + 16
    pad = CONV_K // 2
    acc = None
    for k in range(CONV_K):
        shift = (pad - k) % rows
        shifted = ext if shift == 0 else pltpu.roll(ext, shift, 0)
        term = shifted[8:8 + tb, :] * w_ref[k:k + 1, :]
        acc = term if acc is None else acc + term
    o_ref[...] = _silu(acc + b_ref[...]).astype(o_ref.dtype)


def _conv_call(dm, big, conv_w, conv_b):
    tb = _pick(dm.L, (256, 128))
    cw = _pick(math.gcd(dm.X0, dm.CD), (1024, 512, 256, 128))
    halo = BF16_SUBLANES
    col0 = dm.X0 // cw
    nrb = dm.M // tb
    last_halo = dm.M // halo - 1
    est = 2 * (tb + 2 * halo) * cw * 2 + 2 * tb * cw * 2 + 8 * (tb + 16) * cw * 4
    return pl.pallas_call(
        functools.partial(_conv_kernel, tb=tb, seg_lat=dm.S, seg_ctx=dm.L, n_lat_rows=dm.NL),
        grid=(nrb, dm.CD // cw),
        in_specs=[
            pl.BlockSpec((halo, cw), lambda i, j: (jnp.maximum(i * (tb // halo) - 1, 0), col0 + j)),
            pl.BlockSpec((tb, cw), lambda i, j: (i, col0 + j)),
            pl.BlockSpec((halo, cw), lambda i, j: (jnp.minimum((i + 1) * (tb // halo), last_halo), col0 + j)),
            pl.BlockSpec((8, cw), lambda i, j: (0, j)),
            pl.BlockSpec((1, cw), lambda i, j: (0, j)),
        ],
        out_specs=pl.BlockSpec((tb, cw), lambda i, j: (i, j)),
        out_shape=jax.ShapeDtypeStruct((dm.M, dm.CD), BF16),
        compiler_params=_params(("arbitrary", "arbitrary"), est),
        name="dwconv_silu",
    )(big, big, big, jnp.pad(conv_w, ((0, 8 - CONV_K), (0, 0))), conv_b.reshape(1, dm.CD))


def _split3(v):
    hi = v.astype(BF16)
    r1 = v - hi.astype(F32)
    mid = r1.astype(BF16)
    lo = (r1 - mid.astype(F32)).astype(BF16)
    return hi, mid, lo


def _softplus(v):
    t = jnp.exp(-jnp.abs(v))
    y = 1.0 + t
    return jnp.maximum(v, 0.0) + (jnp.log(y) - ((y - 1.0) - t) / y)


def _ssd_kernel(u_ref, dt_ref, bias_ref, alog_ref, dsk_ref, y_ref, st_ref, *, SH, SG):
    d = pl.program_id(1)
    s = pl.program_id(2)
    hpg = SH // SG
    sd = SH * SSM_HEAD_DIM
    bc = SG * SSM_STATE
    c = CHUNK

    @pl.when(s == 0)
    def _():
        st_ref[...] = jnp.zeros_like(st_ref)

    dt = _softplus(dt_ref[...] + bias_ref[...])
    a = dt * (-jnp.exp(alog_ref[...]))
    row = lax.broadcasted_iota(jnp.int32, (c, c), 0)
    col = lax.broadcasted_iota(jnp.int32, (c, c), 1)
    fwd = d == 0
    tri = jnp.where(fwd, col, row) <= jnp.where(fwd, row, col)
    hi, mid, lo = _split3(a)
    tri_b = jnp.where(tri, 1.0, 0.0).astype(BF16)
    parts = jnp.dot(tri_b, jnp.concatenate([hi, mid, lo], axis=1), preferred_element_type=F32)
    cum = parts[:, :LANES] + parts[:, LANES:2 * LANES] + parts[:, 2 * LANES:]
    tot = jnp.sum(a, axis=0, keepdims=True)
    w_t = (jnp.exp(tot - cum) * dt).T
    cum_t = cum.T
    dt_t = dt.T
    etot = jnp.exp(tot)
    lane_lo = lax.broadcasted_iota(jnp.int32, (c, LANES), 1) < SSM_HEAD_DIM
    lane_lo1 = lax.broadcasted_iota(jnp.int32, (1, LANES), 1) < SSM_HEAD_DIM
    neg_inf = jnp.float32(-jnp.inf)

    for g in range(SG):
        b_g = u_ref[:, sd + g * SSM_STATE: sd + (g + 1) * SSM_STATE]
        c_g = u_ref[:, sd + bc + g * SSM_STATE: sd + bc + (g + 1) * SSM_STATE]
        cb = lax.dot_general(c_g, b_g, (((1,), (1,)), ((), ())), preferred_element_type=F32)
        bt_g = b_g.astype(F32).T
        c_g32 = c_g.astype(F32)
        for pr in range(hpg // 2):
            h0 = g * hpg + 2 * pr
            lo_col, hi_col = h0 * SSM_HEAD_DIM, (h0 + 2) * SSM_HEAD_DIM
            xs_pair = u_ref[:, lo_col:hi_col]
            st_pair = st_ref[:, lo_col:hi_col]
            rhs = jnp.concatenate([xs_pair, st_pair.astype(BF16)], axis=0)
            ys, sts, ets = [], [], []
            for q in range(2):
                h = h0 + q
                colb = jnp.broadcast_to(cum[:, h:h + 1], (c, c))
                decay = jnp.exp(jnp.where(tri, colb - cum_t[h:h + 1, :], neg_inf))
                m_h = decay * cb * dt_t[h:h + 1, :]
                c_h = c_g32 * jnp.exp(colb)
                lhs = jnp.concatenate([m_h, c_h], axis=1).astype(BF16)
                ys.append(jnp.dot(lhs, rhs, preferred_element_type=F32))
                btw = (bt_g * w_t[h:h + 1, :]).astype(BF16)
                sts.append(jnp.dot(btw, xs_pair, preferred_element_type=F32))
                ets.append(jnp.broadcast_to(etot[:, h:h + 1], (1, LANES)))
            y_pair = jnp.where(lane_lo, ys[0], ys[1]) + dsk_ref[:, lo_col:hi_col] * xs_pair.astype(F32)
            e_pair = jnp.where(lane_lo1, ets[0], ets[1])
            st_ref[:, lo_col:hi_col] = e_pair * st_pair + jnp.where(lane_lo, sts[0], sts[1])
            y_ref[:, lo_col:hi_col] = y_pair.astype(y_ref.dtype)


def _ssd_call(dm, u, dt_pad, dt_bias, a_log, d_skip):
    nl, ns = dm.L // CHUNK, dm.S // CHUNK
    lat_blocks = dm.NL // CHUNK

    def rb(b, d, s):
        ctx_chunk = jnp.where(d == 0, s, nl - 1 - s)
        lat_chunk = jnp.where(d == 0, s - nl, ns - 1 - (s - nl))
        return jnp.where(s < nl, lat_blocks + b * nl + ctx_chunk, b * ns + lat_chunk)

    pad = LANES - dm.SH
    bias_p = jnp.pad(dt_bias, ((0, 0), (0, pad))).reshape(N_DIR, 1, LANES)
    alog_p = jnp.pad(a_log, ((0, 0), (0, pad))).reshape(N_DIR, 1, LANES)
    dsk = jnp.repeat(d_skip, SSM_HEAD_DIM)
    dsk_p = jnp.stack([dsk, jnp.zeros_like(dsk)]).reshape(N_DIR, 1, dm.SD)
    est = 2 * CHUNK * dm.CD * 2 + 2 * CHUNK * dm.SD * 2 + SSM_STATE * dm.SD * 4 + (8 << 20)
    return pl.pallas_call(
        functools.partial(_ssd_kernel, SH=dm.SH, SG=dm.SG),
        grid=(dm.B, N_DIR, nl + ns),
        in_specs=[
            pl.BlockSpec((CHUNK, dm.CD), lambda b, d, s: (rb(b, d, s), 0)),
            pl.BlockSpec((CHUNK, LANES), lambda b, d, s: (rb(b, d, s), d)),
            pl.BlockSpec((None, 1, LANES), lambda b, d, s: (d, 0, 0)),
            pl.BlockSpec((None, 1, LANES), lambda b, d, s: (d, 0, 0)),
            pl.BlockSpec((None, 1, dm.SD), lambda b, d, s: (d, 0, 0)),
        ],
        out_specs=pl.BlockSpec((None, CHUNK, dm.SD), lambda b, d, s: (d, rb(b, d, s), 0)),
        out_shape=jax.ShapeDtypeStruct((N_DIR, dm.M, dm.SD), BF16),
        scratch_shapes=[pltpu.VMEM((SSM_STATE, dm.SD), F32)],
        compiler_params=_params(("arbitrary", "arbitrary", "arbitrary"), est),
        name="ssd_scan",
    )(u, dt_pad, bias_p, alog_p, dsk_p)


def _gate_kernel(yf_ref, yb_ref, z_ref, g_ref, o_ref):
    y = yf_ref[...].astype(F32) + yb_ref[...].astype(F32)
    o_ref[...] = _rms(y * _silu(z_ref[...].astype(F32)), g_ref[...]).astype(o_ref.dtype)


def _gate_call(dm, rows, y2, big, ssm_norm):
    tm = _pick(dm.L, (256, 128))
    sd = dm.SD
    zc = dm.AD // sd
    est = 2 * tm * sd * 2 * 4 + 6 * tm * sd * 4
    return pl.pallas_call(
        _gate_kernel,
        grid=(rows // tm,),
        in_specs=[
            pl.BlockSpec((None, tm, sd), lambda i: (0, i, 0)),
            pl.BlockSpec((None, tm, sd), lambda i: (1, i, 0)),
            pl.BlockSpec((tm, sd), lambda i: (i, zc)),
            pl.BlockSpec((1, sd), lambda i: (0, 0)),
        ],
        out_specs=pl.BlockSpec((tm, sd), lambda i: (i, 0)),
        out_shape=jax.ShapeDtypeStruct((rows, sd), BF16),
        compiler_params=_params(("arbitrary",), est),
        name="ssd_gate_norm",
    )(y2, y2, big, ssm_norm.reshape(1, sd))


def _attn_kernel(sink_ref, q_ref, kp_ref, kc_ref, kn_ref, vp_ref, vc_ref, vn_ref, kx_ref, vx_ref, g_ref,
                 o_ref, *, KVH, G, n_lat, L):
    i = pl.program_id(1)
    is_ctx = i >= n_lat
    band = 3 * BLOCK
    width = band + L
    rowq = lax.broadcasted_iota(jnp.int32, (BLOCK, width), 0)
    colk = lax.broadcasted_iota(jnp.int32, (BLOCK, width), 1)
    first_valid = jnp.where(i == 0, BLOCK, 0)
    last_valid = jnp.where(is_ctx, -1, jnp.where(i == n_lat - 1, 2 * BLOCK - 1, band - 1))
    lower = jnp.maximum(rowq, first_valid)
    upper = jnp.minimum(rowq + 2 * WINDOW, last_valid)
    neg_inf = jnp.float32(-jnp.inf)
    in_band = jnp.where(colk >= lower, jnp.where(colk <= upper, 0.0, neg_inf), neg_inf)
    bias = jnp.where(colk >= band, 0.0, in_band)
    scale = HEAD_DIM ** -0.5
    outs = []
    for hk in range(KVH):
        ks = slice(hk * HEAD_DIM, (hk + 1) * HEAD_DIM)
        qh = jnp.concatenate([q_ref[:, (hk * G + g) * HEAD_DIM:(hk * G + g + 1) * HEAD_DIM] for g in range(G)],
                             axis=0)
        kb = jnp.concatenate([kp_ref[:, ks], kc_ref[:, ks], kn_ref[:, ks], kx_ref[:, ks]], axis=0)
        vb = jnp.concatenate([vp_ref[:, ks], vc_ref[:, ks], vn_ref[:, ks], vx_ref[:, ks]], axis=0)
        sc = lax.dot_general(qh, kb, (((1,), (1,)), ((), ())), preferred_element_type=F32) * scale
        ps, dens = [], []
        for g in range(G):
            sg = sc[g * BLOCK:(g + 1) * BLOCK, :] + bias
            sk = sink_ref[hk * G + g]
            mx = jnp.maximum(jnp.max(sg, axis=-1, keepdims=True), sk)
            p = jnp.exp(sg - mx)
            dens.append(jnp.sum(p, axis=-1, keepdims=True) + jnp.exp(sk - mx))
            ps.append(p.astype(BF16))
        pv = jnp.dot(jnp.concatenate(ps, axis=0), vb, preferred_element_type=F32)
        for g in range(G):
            outs.append(pv[g * BLOCK:(g + 1) * BLOCK, :] / dens[g])
    o = jnp.concatenate(outs, axis=1)
    o_ref[...] = _rms(o, g_ref[...]).astype(o_ref.dtype)


def _attn_call(dm, big, sink, attn_norm, with_ctx):
    ns, nl = dm.S // BLOCK, dm.L // BLOCK
    lat_blocks = dm.NL // BLOCK
    steps = ns + (nl if with_ctx else 0)
    rows = dm.M if with_ctx else dm.NL
    kd, ad = dm.KD, dm.AD
    kcol, vcol = dm.K0 // kd, dm.V0 // kd

    def qrow(b, i):
        return jnp.where(i < ns, b * ns + i, lat_blocks + b * nl + (i - ns))

    def band_row(b, i, off):
        return b * ns + jnp.clip(i + off, 0, ns - 1)

    def band_spec(col, off):
        return pl.BlockSpec((BLOCK, kd), lambda b, i: (band_row(b, i, off), col))

    ctx_row0 = dm.NL // dm.L
    est = 2 * BLOCK * ad * 2 * 2 + 2 * 6 * BLOCK * kd * 2 + 2 * 2 * dm.L * kd * 2 + (12 << 20)
    return pl.pallas_call(
        functools.partial(_attn_kernel, KVH=dm.KVH, G=dm.AH // dm.KVH, n_lat=ns, L=dm.L),
        grid=(dm.B, steps),
        in_specs=[
            pl.BlockSpec(memory_space=pltpu.SMEM),
            pl.BlockSpec((BLOCK, ad), lambda b, i: (qrow(b, i), 0)),
            band_spec(kcol, -1), band_spec(kcol, 0), band_spec(kcol, 1),
            band_spec(vcol, -1), band_spec(vcol, 0), band_spec(vcol, 1),
            pl.BlockSpec((dm.L, kd), lambda b, i: (ctx_row0 + b, kcol)),
            pl.BlockSpec((dm.L, kd), lambda b, i: (ctx_row0 + b, vcol)),
            pl.BlockSpec((1, ad), lambda b, i: (0, 0)),
        ],
        out_specs=pl.BlockSpec((BLOCK, ad), lambda b, i: (qrow(b, i), 0)),
        out_shape=jax.ShapeDtypeStruct((rows, ad), BF16),
        compiler_params=_params(("arbitrary", "arbitrary"), est),
        name="band_attention",
    )(sink, big, big, big, big, big, big, big, big, big, attn_norm.reshape(1, ad))


def _router_kernel(h_ref, r_ref, o_ref, *, E):
    x = h_ref[...]
    hi = x.astype(BF16)
    lo = (x - hi.astype(F32)).astype(BF16)
    r = r_ref[...]
    a = jnp.dot(hi, r, preferred_element_type=F32)
    b = jnp.dot(lo, r, preferred_element_type=F32)
    logits = (a[:, :LANES] + a[:, LANES:]) + (b[:, :LANES] + b[:, LANES:])
    lane = lax.broadcasted_iota(jnp.int32, logits.shape, 1)
    neg_inf = jnp.float32(-jnp.inf)
    lg = jnp.where(lane < E, logits, neg_inf)
    m1 = jnp.max(lg, axis=-1, keepdims=True)
    i1 = jnp.min(jnp.where(lg == m1, lane, LANES), axis=-1, keepdims=True)
    lg2 = jnp.where(lane == i1, neg_inf, lg)
    m2 = jnp.max(lg2, axis=-1, keepdims=True)
    i2 = jnp.min(jnp.where(lg2 == m2, lane, LANES), axis=-1, keepdims=True)
    e = jnp.exp(m2 - m1)
    g1 = 1.0 / (1.0 + e)
    g2 = e / (1.0 + e)
    o_ref[...] = jnp.where(lane == 0, i1.astype(F32),
                           jnp.where(lane == 1, i2.astype(F32),
                                     jnp.where(lane == 2, g1, jnp.where(lane == 3, g2, 0.0))))


def _router_call(dm, hf, router):
    tm = _pick(dm.NL, (512, 256, 128))
    r = jnp.pad(router, ((0, 0), (0, LANES - dm.E)))
    r_hi = r.astype(BF16)
    r_lo = (r - r_hi.astype(F32)).astype(BF16)
    r2 = jnp.concatenate([r_hi, r_lo], axis=1)
    est = 2 * tm * dm.D * 4 + 2 * dm.D * 2 * LANES * 2 + 4 * tm * dm.D * 2
    return pl.pallas_call(
        functools.partial(_router_kernel, E=dm.E),
        grid=(dm.NL // tm,),
        in_specs=[pl.BlockSpec((tm, dm.D), lambda i: (i, 0)),
                  pl.BlockSpec((dm.D, 2 * LANES), lambda i: (0, 0))],
        out_specs=pl.BlockSpec((tm, LANES), lambda i: (i, 0)),
        out_shape=jax.ShapeDtypeStruct((dm.NL, LANES), F32),
        compiler_params=_params(("arbitrary",), est),
        name="moe_router",
    )(hf, r2)


def _row_copy(src_hbm, row, dst, slot, sem):
    return pltpu.make_async_copy(src_hbm.at[pl.ds(row, 1), :], dst.at[pl.ds(slot, 1), :], sem)


def _gather_kernel(nt_ref, src_ref, h_hbm, o_ref, buf, sem, *, tm):
    t = pl.program_id(0)

    @pl.when(t < nt_ref[0])
    def _():
        def issue(r, carry):
            _row_copy(h_hbm, src_ref[0, r], buf, r, sem).start()
            return carry

        lax.fori_loop(0, tm, issue, 0)
        pltpu.make_async_copy(h_hbm.at[pl.ds(0, tm), :], buf, sem).wait()
        o_ref[...] = buf[...].astype(o_ref.dtype)

    @pl.when(t >= nt_ref[0])
    def _():
        o_ref[...] = jnp.zeros_like(o_ref)


def _gather_call(dm, hf, src_tok, n_tiles, tm, nt_max):
    grid_spec = pltpu.PrefetchScalarGridSpec(
        num_scalar_prefetch=1, grid=(nt_max,),
        in_specs=[pl.BlockSpec((None, 1, tm), lambda t, nt: (t, 0, 0), memory_space=pltpu.SMEM),
                  pl.BlockSpec(memory_space=pl.ANY)],
        out_specs=pl.BlockSpec((tm, dm.D), lambda t, nt: (t, 0)),
        scratch_shapes=[pltpu.VMEM((tm, dm.D), F32), pltpu.SemaphoreType.DMA(())])
    est = tm * dm.D * 4 + 2 * tm * dm.D * 2 + 2 * tm * dm.D * 4
    return pl.pallas_call(
        functools.partial(_gather_kernel, tm=tm), grid_spec=grid_spec,
        out_shape=jax.ShapeDtypeStruct((nt_max * tm, dm.D), BF16),
        compiler_params=_params(("arbitrary",), est),
        name="moe_gather",
    )(n_tiles, src_tok.reshape(nt_max, 1, tm), hf)


def _combine_kernel(p1_ref, p2_ref, y_hbm, g1_ref, g2_ref, x_ref, gate_ref, gpost_ref, o_ref, buf, sem, *, tm):
    def issue(r, carry):
        _row_copy(y_hbm, p1_ref[0, r], buf.at[0], r, sem).start()
        _row_copy(y_hbm, p2_ref[0, r], buf.at[1], r, sem).start()
        return carry

    lax.fori_loop(0, tm, issue, 0)
    for k in range(2):
        pltpu.make_async_copy(y_hbm.at[pl.ds(0, tm), :], buf.at[k], sem).wait()
    f = g1_ref[...] * buf[0] + g2_ref[...] * buf[1]
    o_ref[...] = x_ref[...] + gate_ref[...] * _rms(f, gpost_ref[...])


def _combine_call(dm, y_sorted, pos1, pos2, g1, g2, x, mod3, gate_lk, g_post, tm):
    d = dm.D
    nblk = dm.NL // tm
    layer, k = gate_lk
    idx_spec = pl.BlockSpec((None, 1, tm), lambda i: (i, 0, 0), memory_space=pltpu.SMEM)
    col_spec = pl.BlockSpec((tm, 1), lambda i: (i, 0))
    est = 2 * tm * d * 4 + 4 * tm * d * 4 + 4 * tm * d * 4 + 4 * tm * LANES * 4
    return pl.pallas_call(
        functools.partial(_combine_kernel, tm=tm),
        grid=(nblk,),
        in_specs=[idx_spec, idx_spec, pl.BlockSpec(memory_space=pl.ANY), col_spec, col_spec,
                  pl.BlockSpec((tm, d), lambda i: (i, 0)),
                  pl.BlockSpec((None, 1, d),
                               lambda i: ((layer * MOD_ROWS + (i * tm) // dm.S) * N_MOD + k, 0, 0)),
                  pl.BlockSpec((1, d), lambda i: (0, 0))],
        out_specs=pl.BlockSpec((tm, d), lambda i: (i, 0)),
        out_shape=jax.ShapeDtypeStruct((dm.NL, d), F32),
        scratch_shapes=[pltpu.VMEM((2, tm, d), F32), pltpu.SemaphoreType.DMA(())],
        compiler_params=_params(("arbitrary",), est),
        name="moe_combine_residual",
    )(pos1.reshape(nblk, 1, tm), pos2.reshape(nblk, 1, tm), y_sorted, g1.reshape(dm.NL, 1),
      g2.reshape(dm.NL, 1), x, mod3, g_post.reshape(1, d))


def _moe_plan(dm, rout, tm):
    t = dm.NL
    e_flat = jnp.concatenate([rout[:, 0], rout[:, 1]]).astype(jnp.int32)
    onehot = (e_flat[:, None] == jnp.arange(dm.E, dtype=jnp.int32)[None, :]).astype(jnp.int32)
    before = jnp.cumsum(onehot, axis=0) - onehot
    rank = jnp.sum(before * onehot, axis=1)
    counts = jnp.sum(onehot, axis=0)
    tiles = (counts + tm - 1) // tm
    tile_end = jnp.cumsum(tiles)
    tile_start = tile_end - tiles
    dest = jnp.sum(onehot * tile_start[None, :], axis=1) * tm + rank
    nt_max = (2 * t) // tm + dm.E
    n_tiles = tile_end[-1:].astype(jnp.int32)
    tile_ids = jnp.arange(nt_max, dtype=jnp.int32)
    tile_expert = jnp.minimum(jnp.sum((tile_ids[:, None] >= tile_end[None, :]).astype(jnp.int32), axis=1),
                              dm.E - 1).astype(jnp.int32)
    tok = jnp.concatenate([jnp.arange(t, dtype=jnp.int32)] * 2)
    src_tok = jnp.zeros((nt_max * tm,), jnp.int32).at[dest].set(tok)
    return src_tok, dest[:t], dest[t:], tile_expert, n_tiles, nt_max


def _rope_tables(dm):
    half = HEAD_DIM // 2
    freqs = ROPE_THETA ** (-jnp.arange(0, half, 2, dtype=F32) / half)
    pos = jnp.arange(dm.S, dtype=jnp.int32)
    row = (pos // dm.GW).astype(F32)[:, None] * freqs
    col = (pos % dm.GW).astype(F32)[:, None] * freqs
    cos = jnp.concatenate([jnp.cos(row), jnp.cos(row), jnp.cos(col), jnp.cos(col)], axis=1)
    sin = jnp.concatenate([-jnp.sin(row), jnp.sin(row), -jnp.sin(col), jnp.sin(col)], axis=1)
    cos = jnp.concatenate([jnp.tile(cos, (dm.B, 1)), jnp.ones((dm.NC, HEAD_DIM), F32)], axis=0)
    sin = jnp.concatenate([jnp.tile(sin, (dm.B, 1)), jnp.zeros((dm.NC, HEAD_DIM), F32)], axis=0)
    return cos, sin


def _trunk(dm, x, c, ctx, c_ctx, ada_w, ada_b, norm_mix_pre, norm_mix_post, norm_ffn_pre, norm_ffn_post,
           w_in, attn_sink, attn_norm, conv_w, conv_b, dt_bias, a_log, d_skip, ssm_norm, w_out,
           ffn_w_gate, ffn_w_up, ffn_w_down, moe_router, moe_w_gate, moe_w_up, moe_w_down):
    depth = ada_w.shape[0]
    d = dm.D
    x2 = x.reshape(dm.NL, d)
    c2 = ctx.reshape(dm.NC, d)
    cvec = jnp.concatenate([c, c_ctx[None, :], jnp.zeros((MOD_ROWS - dm.B - 1, d), F32)], axis=0)
    mod3 = _mod_call(cvec, ada_w, ada_b).reshape(depth * MOD_ROWS * N_MOD, 1, d)
    cos, sin = _rope_tables(dm)

    tm_all = _pick(dm.M, (512, 256, 128))
    tm_lat = _pick(dm.NL, (1024, 512, 256, 128))
    tn_in = _pick(math.gcd(math.gcd(dm.AD, dm.K0), dm.KD), (512, 256, 128))
    rope_panels = tuple(range(dm.AD // tn_in)) + tuple(range(dm.K0 // tn_in, (dm.K0 + dm.KD) // tn_in))
    w_dt = w_in[:, :, dm.DT0:].reshape(depth, d, N_DIR, dm.SH)
    w_dt = jnp.pad(w_dt, ((0, 0), (0, 0), (0, 0), (0, LANES - dm.SH))).reshape(depth, d, N_DIR * LANES)

    xa = None
    for i in range(depth):
        last = i == depth - 1
        if i == 0:
            _, h = _norm_call(dm, dm.M, x2, c2, None, None, None, norm_mix_pre[i], (i, 0), (i, 1), mod3, BF16)
        big = _mm_call([h], [w_in], i, rows=dm.M, tm=tm_all, tn=tn_in, n_cols=dm.DT0, out_dtype=BF16,
                       rope=(cos, sin, rope_panels), name="proj_in")
        dt_pad = _mm_call([h], [w_dt], i, rows=dm.M, tm=tm_all, tn=N_DIR * LANES, n_cols=N_DIR * LANES,
                          out_dtype=F32, name="proj_dt")
        u = _conv_call(dm, big, conv_w[i], conv_b[i])
        y2 = _ssd_call(dm, u, dt_pad, dt_bias[i], a_log[i], d_skip[i])
        rows = dm.NL if last else dm.M
        tm_rows = tm_lat if last else tm_all
        ssm = _gate_call(dm, rows, y2, big, ssm_norm[i])
        attn = _attn_call(dm, big, attn_sink[i], attn_norm[i], with_ctx=not last)
        y = _mm_call([attn, ssm], [w_out], i, rows=rows, tm=tm_rows, tn=_pick(d, (512, 256, 128)), n_cols=d,
                     out_dtype=BF16, name="proj_out")
        if i == 0:
            xa, hf = _norm_call(dm, rows, x2, c2, y, norm_mix_post[i], (i, 2), norm_ffn_pre[i], (i, 3), (i, 4),
                                mod3, BF16 if i % 2 == 0 else F32)
        else:
            xa, hf = _norm_call(dm, rows, xa, None, y, norm_mix_post[i], (i, 2), norm_ffn_pre[i], (i, 3), (i, 4),
                                mod3, BF16 if i % 2 == 0 else F32)
        j = i // 2
        if i % 2 == 0:
            act = _mm_call([hf], [ffn_w_gate, ffn_w_up], j, rows=rows, tm=tm_rows,
                           tn=_pick(dm.DFF, (256, 128)), n_cols=dm.DFF, out_dtype=BF16, mode="swiglu",
                           name="ffn_gate_up")
            f = _mm_call([act], [ffn_w_down], j, rows=rows, tm=_pick(rows, (256, 128)),
                         tn=_pick(d, (256, 128)), n_cols=d, out_dtype=BF16, name="ffn_down")
            if last:
                raise NotImplementedError("dense FFN on the last layer")
            xa, h = _norm_call(dm, rows, xa, None, f, norm_ffn_post[i], (i, 5), norm_mix_pre[i + 1],
                               (i + 1, 0), (i + 1, 1), mod3, BF16)
        else:
            if not last:
                raise NotImplementedError("MoE FFN on a non-final layer")
            tm_e = _pick(dm.NL, (256, 128))
            rout = _router_call(dm, hf, moe_router[j])
            src_tok, pos1, pos2, tile_expert, n_tiles, nt_max = _moe_plan(dm, rout, tm_e)
            xs = _gather_call(dm, hf, src_tok, n_tiles, tm_e, nt_max)
            n_exp = moe_w_gate.shape[1]
            wg = moe_w_gate.reshape((-1,) + moe_w_gate.shape[2:])
            wu = moe_w_up.reshape((-1,) + moe_w_up.shape[2:])
            wd = moe_w_down.reshape((-1,) + moe_w_down.shape[2:])
            act = _mm_call([xs], [wg, wu], j * n_exp, rows=nt_max * tm_e, tm=tm_e,
                           tn=_pick(dm.DE, (256, 128)), n_cols=dm.DE, out_dtype=BF16, mode="swiglu",
                           grouped=(tile_expert, n_tiles), name="moe_gate_up")
            ys = _mm_call([act], [wd], j * n_exp, rows=nt_max * tm_e, tm=tm_e,
                          tn=_pick(d, (512, 256, 128)), n_cols=d, out_dtype=F32,
                          grouped=(tile_expert, n_tiles), name="moe_down")
            xa = _combine_call(dm, ys, pos1, pos2, rout[:, 2], rout[:, 3], xa, mod3, (i, 5), norm_ffn_post[i],
                               tm_e)
    return xa[:dm.NL].reshape(dm.B, dm.S, d)


def kernel(x, c, ctx, c_ctx, ada_w, ada_b, norm_mix_pre, norm_mix_post, norm_ffn_pre, norm_ffn_post, w_in,
           attn_sink, attn_norm, conv_w, conv_b, dt_bias, a_log, d_skip, ssm_norm, w_out, ffn_w_gate, ffn_w_up,
           ffn_w_down, moe_router, moe_w_gate, moe_w_up, moe_w_down):
    b, s, d = x.shape
    sh = a_log.shape[2]
    cd = conv_w.shape[2]
    sd = sh * SSM_HEAD_DIM
    dm = Dims(B=b, S=s, L=ctx.shape[1], D=d, AH=attn_sink.shape[1], KVH=ATTN_KV_HEADS, SH=sh,
              SG=(cd - sd) // (2 * SSM_STATE), DFF=ffn_w_gate.shape[2], E=moe_router.shape[2],
              DE=moe_w_gate.shape[3], GW=GRID_W)
    assert w_in.shape[2] == dm.DT0 + N_DIR * dm.SH and w_out.shape[1] == dm.AD + dm.SD and dm.AD == dm.SD
    assert dm.S % BLOCK == 0 and dm.L % BLOCK == 0 and dm.SH <= LANES and dm.E <= LANES
    return _trunk(dm, x, c, ctx, c_ctx, ada_w, ada_b, norm_mix_pre, norm_mix_post, norm_ffn_pre, norm_ffn_post,
                  w_in, attn_sink, attn_norm, conv_w, conv_b, dt_bias, a_log, d_skip, ssm_norm, w_out,
                  ffn_w_gate, ffn_w_up, ffn_w_down, moe_router, moe_w_gate, moe_w_up, moe_w_down)
```

```python
import functools
import math
from typing import NamedTuple

import jax
import jax.numpy as jnp
from jax import lax
from jax.experimental import pallas as pl
from jax.experimental.pallas import tpu as pltpu

F32 = jnp.float32
BF16 = jnp.bfloat16

HEAD_DIM = 128
WINDOW = 128
BLOCK = 128
ROPE_THETA = 10000.0
SSM_HEAD_DIM = 64
SSM_STATE = 128
CONV_K = 5
CHUNK = 128
N_DIR = 2
GRID_W = 64
ATTN_KV_HEADS = 4
N_MOD = 6
EPS = 1e-6
MOD_ROWS = 8

V7X_VMEM_BYTES = 64 * 1024 * 1024
VMEM_LIMIT_CAP = 56 * 1024 * 1024
LANES = 128
BF16_SUBLANES = 16
MM_ROW_TILE_CAP = 1100


class Dims(NamedTuple):
    B: int
    S: int
    L: int
    D: int
    AH: int
    KVH: int
    SH: int
    SG: int
    DFF: int
    E: int
    DE: int
    GW: int

    @property
    def AD(self):
        return self.AH * HEAD_DIM

    @property
    def KD(self):
        return self.KVH * HEAD_DIM

    @property
    def SD(self):
        return self.SH * SSM_HEAD_DIM

    @property
    def BC(self):
        return self.SG * SSM_STATE

    @property
    def CD(self):
        return self.SD + 2 * self.BC

    @property
    def K0(self):
        return self.AD + self.SD

    @property
    def V0(self):
        return self.K0 + self.KD

    @property
    def X0(self):
        return self.V0 + self.KD

    @property
    def DT0(self):
        return self.X0 + self.CD

    @property
    def NL(self):
        return self.B * self.S

    @property
    def NC(self):
        return self.B * self.L

    @property
    def M(self):
        return self.NL + self.NC


def _pick(n, prefs):
    for p in prefs:
        if n % p == 0:
            return p
    raise ValueError(f"no tile in {prefs} divides {n}")


def _row_tile(rows, cap):
    for t in range(min(cap, rows), 0, -1):
        if rows % t == 0 and t % BF16_SUBLANES == 0:
            return t
    raise ValueError(f"no row tile <= {cap} divides {rows}")


def _params(sem, est_bytes):
    limit = int(min(max(est_bytes * 5 // 4 + (2 << 20), 16 << 20), VMEM_LIMIT_CAP))
    return pltpu.CompilerParams(dimension_semantics=sem, vmem_limit_bytes=limit)


def _silu(v):
    return v / (1.0 + jnp.exp(-v))


def _rms(v, g):
    return v * lax.rsqrt(jnp.mean(v * v, axis=-1, keepdims=True) + EPS) * g


def _mod_kernel(c_ref, w_ref, b_ref, o_ref):
    c = _silu(c_ref[...]).astype(BF16)
    o_ref[...] = jnp.dot(c, w_ref[...].astype(BF16), preferred_element_type=F32) + b_ref[...]


def _mod_call(cvec, ada_w, ada_b):
    depth, d, n = ada_w.shape
    tn = _pick(n, (512, 256, 128))
    est = 2 * d * tn * 4 + d * tn * 2 + 4 * MOD_ROWS * tn * 4
    return pl.pallas_call(
        _mod_kernel,
        grid=(depth, n // tn),
        in_specs=[
            pl.BlockSpec((MOD_ROWS, d), lambda l, j: (0, 0)),
            pl.BlockSpec((None, d, tn), lambda l, j: (l, 0, j)),
            pl.BlockSpec((None, 1, tn), lambda l, j: (l, 0, j)),
        ],
        out_specs=pl.BlockSpec((None, MOD_ROWS, tn), lambda l, j: (l, 0, j)),
        out_shape=jax.ShapeDtypeStruct((depth, MOD_ROWS, n), F32),
        compiler_params=_params(("arbitrary", "arbitrary"), est),
        name="adaln_mod",
    )(cvec, ada_w, ada_b.reshape(depth, 1, n))


def _norm_kernel(*refs, split, has_res, n_lat_blocks):
    refs = list(refs)
    x_ref = refs.pop(0)
    c_ref = refs.pop(0) if split else None
    if has_res:
        y_ref, gpost_ref, gate_ref = refs.pop(0), refs.pop(0), refs.pop(0)
    gpre_ref, sh_ref, sc_ref = refs.pop(0), refs.pop(0), refs.pop(0)
    if has_res:
        xo_ref = refs.pop(0)
    h_ref = refs.pop(0)

    def run(src_ref):
        x = src_ref[...]
        if has_res:
            x = x + gate_ref[...] * _rms(y_ref[...].astype(F32), gpost_ref[...])
            xo_ref[...] = x
        h_ref[...] = (_rms(x, gpre_ref[...]) * (1.0 + sc_ref[...]) + sh_ref[...]).astype(h_ref.dtype)

    if split:
        i = pl.program_id(0)
        pl.when(i < n_lat_blocks)(lambda: run(x_ref))
        pl.when(i >= n_lat_blocks)(lambda: run(c_ref))
    else:
        run(x_ref)


def _norm_call(dm, rows, x, ctx2d, y, g_post, gate_lk, g_pre, sh_lk, sc_lk, mod3, h_dtype):
    d = dm.D
    tm = _pick(dm.L, (256, 128))
    nblk = rows // tm
    nlat = dm.NL // tm
    split = ctx2d is not None
    has_res = y is not None

    def mod_spec(lk):
        layer, k = lk
        return pl.BlockSpec(
            (None, 1, d),
            lambda i: ((layer * MOD_ROWS + jnp.minimum((i * tm) // dm.S, dm.B)) * N_MOD + k, 0, 0))

    row_spec = pl.BlockSpec((tm, d), lambda i: (i, 0))
    vec_spec = pl.BlockSpec((1, d), lambda i: (0, 0))
    in_specs, args = [], []
    if split:
        in_specs += [pl.BlockSpec((tm, d), lambda i: (jnp.minimum(i, nlat - 1), 0)),
                     pl.BlockSpec((tm, d), lambda i: (jnp.maximum(i - nlat, 0), 0))]
        args += [x, ctx2d]
    else:
        in_specs += [row_spec]
        args += [x]
    if has_res:
        in_specs += [row_spec, vec_spec, mod_spec(gate_lk)]
        args += [y, g_post.reshape(1, d), mod3]
    in_specs += [vec_spec, mod_spec(sh_lk), mod_spec(sc_lk)]
    args += [g_pre.reshape(1, d), mod3, mod3]
    out_specs, out_shape = [], []
    if has_res:
        out_specs.append(row_spec)
        out_shape.append(jax.ShapeDtypeStruct((rows, d), F32))
    out_specs.append(row_spec)
    out_shape.append(jax.ShapeDtypeStruct((rows, d), h_dtype))
    est = 2 * tm * d * (4 + 4 + 2 + 4 + 4)
    outs = pl.pallas_call(
        functools.partial(_norm_kernel, split=split, has_res=has_res, n_lat_blocks=nlat),
        grid=(nblk,),
        in_specs=in_specs,
        out_specs=out_specs,
        out_shape=out_shape,
        compiler_params=_params(("arbitrary",), est),
        name="residual_prenorm" if has_res else "prenorm",
    )(*args)
    return outs if has_res else (None, outs[0])


def _rope(acc, cos, sin_signed, tn):
    lane = lax.broadcasted_iota(jnp.int32, acc.shape, 1)
    first = (lane % 64) < 32
    partner = jnp.where(first, pltpu.roll(acc, tn - 32, 1), pltpu.roll(acc, 32, 1))
    reps = tn // HEAD_DIM
    cos_t = jnp.concatenate([cos] * reps, axis=1) if reps > 1 else cos
    sin_t = jnp.concatenate([sin_signed] * reps, axis=1) if reps > 1 else sin_signed
    return acc * cos_t + partner * sin_t


def _mm_kernel(*refs, n_lhs, k_sizes, n_w, mode, with_rope, grouped, tn, w_nt):
    refs = list(refs)
    if grouped:
        te_ref, nt_ref = refs.pop(0), refs.pop(0)
    lhs_refs = [refs.pop(0) for _ in range(n_lhs)]
    w_refs = [refs.pop(0) for _ in range(n_w)]
    if with_rope:
        cos_ref, sin_ref = refs.pop(0), refs.pop(0)
    o_ref = refs.pop(0)
    wb_refs = [refs.pop(0) for _ in range(n_w)]
    j = pl.program_id(0)
    m = pl.program_id(1)

    if grouped:
        nt = nt_ref[0]
        mc = jnp.minimum(m, nt - 1)
        prev = jnp.maximum(mc - 1, 0)
        new_panel = jnp.logical_or(m == 0, te_ref[mc] != te_ref[prev])
        active = m < nt
    else:
        new_panel = m == 0
        active = None

    @pl.when(new_panel)
    def _():
        for w_ref, wb_ref in zip(w_refs, wb_refs):
            wb_ref[...] = w_ref[...].astype(BF16)

    def compute():
        accs = []
        for wb_ref in wb_refs:
            acc, k0 = None, 0
            for l_ref, ks in zip(lhs_refs, k_sizes):
                if w_nt:
                    part = lax.dot_general(l_ref[...], wb_ref[:, k0:k0 + ks], (((1,), (1,)), ((), ())),
                                           preferred_element_type=F32)
                else:
                    part = jnp.dot(l_ref[...], wb_ref[k0:k0 + ks, :], preferred_element_type=F32)
                acc = part if acc is None else acc + part
                k0 += ks
            accs.append(acc)
        res = _silu(accs[0]) * accs[1] if mode == "swiglu" else accs[0]
        if with_rope:
            res = _rope(res, cos_ref[...], sin_ref[...], tn)
        o_ref[...] = res.astype(o_ref.dtype)

    if grouped:
        pl.when(active)(compute)

        @pl.when(jnp.logical_not(active))
        def _():
            o_ref[...] = jnp.zeros_like(o_ref)
    else:
        compute()


def _mm_call(lhs_list, w_list, w_group, *, rows, tm, tn, n_cols, out_dtype, mode="plain",
             rope=None, grouped=None, w_nt=False, w_buffers=2, panel_src=None, name="panel_matmul"):
    k_sizes = tuple(a.shape[1] for a in lhs_list)
    k_tot = sum(k_sizes)
    n_w = len(w_list)
    assert all(w.shape[2 if w_nt else 1] == k_tot for w in w_list)
    n_panels = n_cols // tn
    n_blocks = rows // tm
    args, in_specs = [], []
    src = panel_src if panel_src is not None else (lambda j: j)
    w_pos = (lambda g, j: (g, src(j), 0)) if w_nt else (lambda g, j: (g, 0, src(j)))
    if grouped is not None:
        tile_expert, n_tiles = grouped
        row_idx = lambda j, m, te, nt: jnp.minimum(m, nt[0] - 1)
        lhs_map = lambda j, m, te, nt: (row_idx(j, m, te, nt), 0)
        w_map = lambda j, m, te, nt: w_pos(w_group + te[row_idx(j, m, te, nt)], j)
        out_map = lambda j, m, te, nt: (m, j)
    else:
        lhs_map = lambda j, m: (m, 0)
        w_map = lambda j, m: w_pos(w_group, j)
        out_map = lambda j, m: (m, j)
    for a, ks in zip(lhs_list, k_sizes):
        in_specs.append(pl.BlockSpec((tm, ks), lhs_map))
        args.append(a)
    w_block = (None, tn, k_tot) if w_nt else (None, k_tot, tn)
    w_mode = {} if w_buffers == 2 else dict(pipeline_mode=pl.Buffered(w_buffers))
    for w in w_list:
        in_specs.append(pl.BlockSpec(w_block, w_map, **w_mode))
        args.append(w)
    if rope is not None:
        cos, sin_signed = rope
        in_specs += [pl.BlockSpec((tm, HEAD_DIM), lambda j, m: (m, 0))] * 2
        args += [cos, sin_signed]
    out_bytes = jnp.dtype(out_dtype).itemsize
    est = (n_w * (w_buffers * k_tot * tn * 4 + k_tot * tn * 2) + 2 * tm * k_tot * 2 + 2 * tm * tn * out_bytes
           + (2 + n_w) * tm * tn * 4)
    kern = functools.partial(_mm_kernel, n_lhs=len(lhs_list), k_sizes=k_sizes, n_w=n_w, mode=mode,
                             with_rope=rope is not None, grouped=grouped is not None, tn=tn, w_nt=w_nt)
    scratch = [pltpu.VMEM(w_block[1:], BF16) for _ in range(n_w)]
    out_shape = jax.ShapeDtypeStruct((rows, n_cols), out_dtype)
    cp = _params(("arbitrary", "arbitrary"), est)
    if grouped is not None:
        grid_spec = pltpu.PrefetchScalarGridSpec(
            num_scalar_prefetch=2, grid=(n_panels, n_blocks), in_specs=in_specs,
            out_specs=pl.BlockSpec((tm, tn), out_map), scratch_shapes=scratch)
        return pl.pallas_call(kern, grid_spec=grid_spec, out_shape=out_shape, compiler_params=cp,
                              name=name)(tile_expert, n_tiles, *args)
    return pl.pallas_call(kern, grid=(n_panels, n_blocks), in_specs=in_specs,
                          out_specs=pl.BlockSpec((tm, tn), out_map), out_shape=out_shape,
                          scratch_shapes=scratch, compiler_params=cp, name=name)(*args)


def _conv_kernel(prev_ref, cur_ref, next_ref, w_ref, b_ref, o_ref, *, tb, seg_lat, seg_ctx, n_lat_rows):
    i = pl.program_id(0)
    row0 = i * tb
    in_lat = row0 < n_lat_rows
    off = jnp.where(in_lat, row0 % seg_lat, (row0 - n_lat_rows) % seg_ctx)
    seg = jnp.where(in_lat, seg_lat, seg_ctx)
    keep_prev = jnp.where(off != 0, 1.0, 0.0)
    keep_next = jnp.where(off + tb != seg, 1.0, 0.0)
    halo = BF16_SUBLANES
    prev = prev_ref[...].astype(F32)[halo - 8:, :] * keep_prev
    nxt = next_ref[...].astype(F32)[:8, :] * keep_next
    ext = jnp.concatenate([prev, cur_ref[...].astype(F32), nxt], axis=0)
    rows = tb + 16
    pad = CONV_K // 2
    acc = None
    for k in range(CONV_K):
        shift = (pad - k) % rows
        shifted = ext if shift == 0 else pltpu.roll(ext, shift, 0)
        term = shifted[8:8 + tb, :] * w_ref[k:k + 1, :]
        acc = term if acc is None else acc + term
    o_ref[...] = _silu(acc + b_ref[...]).astype(o_ref.dtype)


def _conv_call(dm, big, conv_w, conv_b):
    tb = _pick(dm.L, (256, 128))
    cw = _pick(math.gcd(dm.SD, dm.CD), (1024, 512, 256, 128))
    halo = BF16_SUBLANES
    col0 = dm.SD // cw
    nrb = dm.M // tb
    last_halo = dm.M // halo - 1
    est = 2 * (tb + 2 * halo) * cw * 2 + 2 * tb * cw * 2 + 8 * (tb + 16) * cw * 4
    return pl.pallas_call(
        functools.partial(_conv_kernel, tb=tb, seg_lat=dm.S, seg_ctx=dm.L, n_lat_rows=dm.NL),
        grid=(nrb, dm.CD // cw),
        in_specs=[
            pl.BlockSpec((halo, cw), lambda i, j: (jnp.maximum(i * (tb // halo) - 1, 0), col0 + j)),
            pl.BlockSpec((tb, cw), lambda i, j: (i, col0 + j)),
            pl.BlockSpec((halo, cw), lambda i, j: (jnp.minimum((i + 1) * (tb // halo), last_halo), col0 + j)),
            pl.BlockSpec((8, cw), lambda i, j: (0, j)),
            pl.BlockSpec((1, cw), lambda i, j: (0, j)),
        ],
        out_specs=pl.BlockSpec((tb, cw), lambda i, j: (i, j)),
        out_shape=jax.ShapeDtypeStruct((dm.M, dm.CD), BF16),
        compiler_params=_params(("arbitrary", "arbitrary"), est),
        name="dwconv_silu",
    )(big, big, big, jnp.pad(conv_w, ((0, 8 - CONV_K), (0, 0))), conv_b.reshape(1, dm.CD))


def _split3(v):
    hi = v.astype(BF16)
    r1 = v - hi.astype(F32)
    mid = r1.astype(BF16)
    lo = (r1 - mid.astype(F32)).astype(BF16)
    return hi, mid, lo


def _softplus(v):
    t = jnp.exp(-jnp.abs(v))
    y = 1.0 + t
    return jnp.maximum(v, 0.0) + (jnp.log(y) - ((y - 1.0) - t) / y)


def _ssd_kernel(u_ref, dt_ref, bias_ref, alog_ref, dsk_ref, y_ref, st_ref, *, SH, SG):
    d = pl.program_id(1)
    s = pl.program_id(2)
    hpg = SH // SG
    sd = SH * SSM_HEAD_DIM
    bc = SG * SSM_STATE
    c = CHUNK

    @pl.when(s == 0)
    def _():
        st_ref[...] = jnp.zeros_like(st_ref)

    dt = _softplus(dt_ref[...] + bias_ref[...])
    a = dt * (-jnp.exp(alog_ref[...]))
    row = lax.broadcasted_iota(jnp.int32, (c, c), 0)
    col = lax.broadcasted_iota(jnp.int32, (c, c), 1)
    fwd = d == 0
    tri = jnp.where(fwd, col, row) <= jnp.where(fwd, row, col)
    hi, mid, lo = _split3(a)
    tri_b = jnp.where(tri, 1.0, 0.0).astype(BF16)
    parts = jnp.dot(tri_b, jnp.concatenate([hi, mid, lo], axis=1), preferred_element_type=F32)
    cum = parts[:, :LANES] + parts[:, LANES:2 * LANES] + parts[:, 2 * LANES:]
    tot = jnp.sum(a, axis=0, keepdims=True)
    w_t = (jnp.exp(tot - cum) * dt).T
    cum_t = cum.T
    dt_t = dt.T
    etot = jnp.exp(tot)
    lane_lo = lax.broadcasted_iota(jnp.int32, (c, LANES), 1) < SSM_HEAD_DIM
    lane_lo1 = lax.broadcasted_iota(jnp.int32, (1, LANES), 1) < SSM_HEAD_DIM
    neg_inf = jnp.float32(-jnp.inf)

    for g in range(SG):
        b_g = u_ref[:, sd + g * SSM_STATE: sd + (g + 1) * SSM_STATE]
        c_g = u_ref[:, sd + bc + g * SSM_STATE: sd + bc + (g + 1) * SSM_STATE]
        cb = lax.dot_general(c_g, b_g, (((1,), (1,)), ((), ())), preferred_element_type=F32)
        bt_g = b_g.astype(F32).T
        c_g32 = c_g.astype(F32)
        for pr in range(hpg // 2):
            h0 = g * hpg + 2 * pr
            lo_col, hi_col = h0 * SSM_HEAD_DIM, (h0 + 2) * SSM_HEAD_DIM
            xs_pair = u_ref[:, lo_col:hi_col]
            st_pair = st_ref[:, lo_col:hi_col]
            rhs = jnp.concatenate([xs_pair, st_pair.astype(BF16)], axis=0)
            ys, sts, ets = [], [], []
            for q in range(2):
                h = h0 + q
                colb = jnp.broadcast_to(cum[:, h:h + 1], (c, c))
                decay = jnp.exp(jnp.where(tri, colb - cum_t[h:h + 1, :], neg_inf))
                m_h = decay * cb * dt_t[h:h + 1, :]
                c_h = c_g32 * jnp.exp(colb)
                lhs = jnp.concatenate([m_h, c_h], axis=1).astype(BF16)
                ys.append(jnp.dot(lhs, rhs, preferred_element_type=F32))
                btw = (bt_g * w_t[h:h + 1, :]).astype(BF16)
                sts.append(jnp.dot(btw, xs_pair, preferred_element_type=F32))
                ets.append(jnp.broadcast_to(etot[:, h:h + 1], (1, LANES)))
            y_pair = jnp.where(lane_lo, ys[0], ys[1]) + dsk_ref[:, lo_col:hi_col] * xs_pair.astype(F32)
            e_pair = jnp.where(lane_lo1, ets[0], ets[1])
            st_ref[:, lo_col:hi_col] = e_pair * st_pair + jnp.where(lane_lo, sts[0], sts[1])
            y_ref[:, lo_col:hi_col] = y_pair.astype(y_ref.dtype)


def _ssd_call(dm, u, dt_pad, dt_bias, a_log, d_skip):
    nl, ns = dm.L // CHUNK, dm.S // CHUNK
    lat_blocks = dm.NL // CHUNK

    def rb(b, d, s):
        ctx_chunk = jnp.where(d == 0, s, nl - 1 - s)
        lat_chunk = jnp.where(d == 0, s - nl, ns - 1 - (s - nl))
        return jnp.where(s < nl, lat_blocks + b * nl + ctx_chunk, b * ns + lat_chunk)

    pad = LANES - dm.SH
    bias_p = jnp.pad(dt_bias, ((0, 0), (0, pad))).reshape(N_DIR, 1, LANES)
    alog_p = jnp.pad(a_log, ((0, 0), (0, pad))).reshape(N_DIR, 1, LANES)
    dsk = jnp.repeat(d_skip, SSM_HEAD_DIM)
    dsk_p = jnp.stack([dsk, jnp.zeros_like(dsk)]).reshape(N_DIR, 1, dm.SD)
    est = 2 * CHUNK * dm.CD * 2 + 2 * CHUNK * dm.SD * 2 + SSM_STATE * dm.SD * 4 + (8 << 20)
    return pl.pallas_call(
        functools.partial(_ssd_kernel, SH=dm.SH, SG=dm.SG),
        grid=(dm.B, N_DIR, nl + ns),
        in_specs=[
            pl.BlockSpec((CHUNK, dm.CD), lambda b, d, s: (rb(b, d, s), 0)),
            pl.BlockSpec((CHUNK, LANES), lambda b, d, s: (rb(b, d, s), d)),
            pl.BlockSpec((None, 1, LANES), lambda b, d, s: (d, 0, 0)),
            pl.BlockSpec((None, 1, LANES), lambda b, d, s: (d, 0, 0)),
            pl.BlockSpec((None, 1, dm.SD), lambda b, d, s: (d, 0, 0)),
        ],
        out_specs=pl.BlockSpec((None, CHUNK, dm.SD), lambda b, d, s: (d, rb(b, d, s), 0)),
        out_shape=jax.ShapeDtypeStruct((N_DIR, dm.M, dm.SD), BF16),
        scratch_shapes=[pltpu.VMEM((SSM_STATE, dm.SD), F32)],
        compiler_params=_params(("arbitrary", "arbitrary", "arbitrary"), est),
        name="ssd_scan",
    )(u, dt_pad, bias_p, alog_p, dsk_p)


def _gate_kernel(yf_ref, yb_ref, z_ref, g_ref, o_ref):
    y = yf_ref[...].astype(F32) + yb_ref[...].astype(F32)
    o_ref[...] = _rms(y * _silu(z_ref[...].astype(F32)), g_ref[...]).astype(o_ref.dtype)


def _gate_call(dm, rows, y2, big, ssm_norm):
    tm = _pick(dm.L, (256, 128))
    sd = dm.SD
    zc = 0
    est = 2 * tm * sd * 2 * 4 + 6 * tm * sd * 4
    return pl.pallas_call(
        _gate_kernel,
        grid=(rows // tm,),
        in_specs=[
            pl.BlockSpec((None, tm, sd), lambda i: (0, i, 0)),
            pl.BlockSpec((None, tm, sd), lambda i: (1, i, 0)),
            pl.BlockSpec((tm, sd), lambda i: (i, zc)),
            pl.BlockSpec((1, sd), lambda i: (0, 0)),
        ],
        out_specs=pl.BlockSpec((tm, sd), lambda i: (i, 0)),
        out_shape=jax.ShapeDtypeStruct((rows, sd), BF16),
        compiler_params=_params(("arbitrary",), est),
        name="ssd_gate_norm",
    )(y2, y2, big, ssm_norm.reshape(1, sd))


def _attn_kernel(sink_ref, q_ref, kp_ref, kc_ref, kn_ref, vp_ref, vc_ref, vn_ref, kx_ref, vx_ref, g_ref,
                 o_ref, *, KVH, G, n_lat, L):
    i = pl.program_id(1)
    is_ctx = i >= n_lat
    band = 3 * BLOCK
    width = band + L
    rowq = lax.broadcasted_iota(jnp.int32, (BLOCK, width), 0)
    colk = lax.broadcasted_iota(jnp.int32, (BLOCK, width), 1)
    first_valid = jnp.where(i == 0, BLOCK, 0)
    last_valid = jnp.where(is_ctx, -1, jnp.where(i == n_lat - 1, 2 * BLOCK - 1, band - 1))
    lower = jnp.maximum(rowq, first_valid)
    upper = jnp.minimum(rowq + 2 * WINDOW, last_valid)
    neg_inf = jnp.float32(-jnp.inf)
    in_band = jnp.where(colk >= lower, jnp.where(colk <= upper, 0.0, neg_inf), neg_inf)
    bias = jnp.where(colk >= band, 0.0, in_band)
    scale = HEAD_DIM ** -0.5
    outs = []
    for hk in range(KVH):
        ks = slice(hk * HEAD_DIM, (hk + 1) * HEAD_DIM)
        qh = jnp.concatenate([q_ref[:, (hk * G + g) * HEAD_DIM:(hk * G + g + 1) * HEAD_DIM] for g in range(G)],
                             axis=0)
        kb = jnp.concatenate([kp_ref[:, ks], kc_ref[:, ks], kn_ref[:, ks], kx_ref[:, ks]], axis=0)
        vb = jnp.concatenate([vp_ref[:, ks], vc_ref[:, ks], vn_ref[:, ks], vx_ref[:, ks]], axis=0)
        sc = lax.dot_general(qh, kb, (((1,), (1,)), ((), ())), preferred_element_type=F32) * scale
        ps, dens = [], []
        for g in range(G):
            sg = sc[g * BLOCK:(g + 1) * BLOCK, :] + bias
            sk = sink_ref[hk * G + g]
            mx = jnp.maximum(jnp.max(sg, axis=-1, keepdims=True), sk)
            p = jnp.exp(sg - mx)
            dens.append(jnp.sum(p, axis=-1, keepdims=True) + jnp.exp(sk - mx))
            ps.append(p.astype(BF16))
        pv = jnp.dot(jnp.concatenate(ps, axis=0), vb, preferred_element_type=F32)
        for g in range(G):
            outs.append(pv[g * BLOCK:(g + 1) * BLOCK, :] / dens[g])
    o = jnp.concatenate(outs, axis=1)
    o_ref[...] = _rms(o, g_ref[...]).astype(o_ref.dtype)


def _attn_call(dm, qk, big, sink, attn_norm, with_ctx):
    ns, nl = dm.S // BLOCK, dm.L // BLOCK
    lat_blocks = dm.NL // BLOCK
    steps = ns + (nl if with_ctx else 0)
    rows = dm.M if with_ctx else dm.NL
    kd, ad = dm.KD, dm.AD
    kcol, vcol = ad // kd, (dm.SD + dm.CD) // kd

    def qrow(b, i):
        return jnp.where(i < ns, b * ns + i, lat_blocks + b * nl + (i - ns))

    def band_row(b, i, off):
        return b * ns + jnp.clip(i + off, 0, ns - 1)

    def band_spec(col, off):
        return pl.BlockSpec((BLOCK, kd), lambda b, i: (band_row(b, i, off), col))

    ctx_row0 = dm.NL // dm.L
    est = 2 * BLOCK * ad * 2 * 2 + 2 * 6 * BLOCK * kd * 2 + 2 * 2 * dm.L * kd * 2 + (12 << 20)
    return pl.pallas_call(
        functools.partial(_attn_kernel, KVH=dm.KVH, G=dm.AH // dm.KVH, n_lat=ns, L=dm.L),
        grid=(dm.B, steps),
        in_specs=[
            pl.BlockSpec(memory_space=pltpu.SMEM),
            pl.BlockSpec((BLOCK, ad), lambda b, i: (qrow(b, i), 0)),
            band_spec(kcol, -1), band_spec(kcol, 0), band_spec(kcol, 1),
            band_spec(vcol, -1), band_spec(vcol, 0), band_spec(vcol, 1),
            pl.BlockSpec((dm.L, kd), lambda b, i: (ctx_row0 + b, kcol)),
            pl.BlockSpec((dm.L, kd), lambda b, i: (ctx_row0 + b, vcol)),
            pl.BlockSpec((1, ad), lambda b, i: (0, 0)),
        ],
        out_specs=pl.BlockSpec((BLOCK, ad), lambda b, i: (qrow(b, i), 0)),
        out_shape=jax.ShapeDtypeStruct((rows, ad), BF16),
        compiler_params=_params(("arbitrary", "arbitrary"), est),
        name="band_attention",
    )(sink, qk, qk, qk, qk, big, big, big, qk, big, attn_norm.reshape(1, ad))


def _router_kernel(h_ref, r_ref, o_ref, *, E):
    x = h_ref[...]
    hi = x.astype(BF16)
    lo = (x - hi.astype(F32)).astype(BF16)
    r = r_ref[...]
    a = jnp.dot(hi, r, preferred_element_type=F32)
    b = jnp.dot(lo, r, preferred_element_type=F32)
    logits = (a[:, :LANES] + a[:, LANES:]) + (b[:, :LANES] + b[:, LANES:])
    lane = lax.broadcasted_iota(jnp.int32, logits.shape, 1)
    neg_inf = jnp.float32(-jnp.inf)
    lg = jnp.where(lane < E, logits, neg_inf)
    m1 = jnp.max(lg, axis=-1, keepdims=True)
    i1 = jnp.min(jnp.where(lg == m1, lane, LANES), axis=-1, keepdims=True)
    lg2 = jnp.where(lane == i1, neg_inf, lg)
    m2 = jnp.max(lg2, axis=-1, keepdims=True)
    i2 = jnp.min(jnp.where(lg2 == m2, lane, LANES), axis=-1, keepdims=True)
    e = jnp.exp(m2 - m1)
    g1 = 1.0 / (1.0 + e)
    g2 = e / (1.0 + e)
    o_ref[...] = jnp.where(lane == 0, i1.astype(F32),
                           jnp.where(lane == 1, i2.astype(F32),
                                     jnp.where(lane == 2, g1, jnp.where(lane == 3, g2, 0.0))))


def _router_call(dm, hf, router):
    tm = _pick(dm.NL, (512, 256, 128))
    r = jnp.pad(router, ((0, 0), (0, LANES - dm.E)))
    r_hi = r.astype(BF16)
    r_lo = (r - r_hi.astype(F32)).astype(BF16)
    r2 = jnp.concatenate([r_hi, r_lo], axis=1)
    est = 2 * tm * dm.D * 4 + 2 * dm.D * 2 * LANES * 2 + 4 * tm * dm.D * 2
    return pl.pallas_call(
        functools.partial(_router_kernel, E=dm.E),
        grid=(dm.NL // tm,),
        in_specs=[pl.BlockSpec((tm, dm.D), lambda i: (i, 0)),
                  pl.BlockSpec((dm.D, 2 * LANES), lambda i: (0, 0))],
        out_specs=pl.BlockSpec((tm, LANES), lambda i: (i, 0)),
        out_shape=jax.ShapeDtypeStruct((dm.NL, LANES), F32),
        compiler_params=_params(("arbitrary",), est),
        name="moe_router",
    )(hf, r2)


def _row_copy(src_hbm, row, dst, slot, sem):
    return pltpu.make_async_copy(src_hbm.at[pl.ds(row, 1), :], dst.at[pl.ds(slot, 1), :], sem)


def _gather_kernel(nt_ref, src_ref, h_hbm, o_ref, buf, sem, *, tm):
    t = pl.program_id(0)

    @pl.when(t < nt_ref[0])
    def _():
        def issue(r, carry):
            _row_copy(h_hbm, src_ref[0, r], buf, r, sem).start()
            return carry

        lax.fori_loop(0, tm, issue, 0)
        pltpu.make_async_copy(h_hbm.at[pl.ds(0, tm), :], buf, sem).wait()
        o_ref[...] = buf[...].astype(o_ref.dtype)

    @pl.when(t >= nt_ref[0])
    def _():
        o_ref[...] = jnp.zeros_like(o_ref)


def _gather_call(dm, hf, src_tok, n_tiles, tm, nt_max):
    grid_spec = pltpu.PrefetchScalarGridSpec(
        num_scalar_prefetch=1, grid=(nt_max,),
        in_specs=[pl.BlockSpec((None, 1, tm), lambda t, nt: (t, 0, 0), memory_space=pltpu.SMEM),
                  pl.BlockSpec(memory_space=pl.ANY)],
        out_specs=pl.BlockSpec((tm, dm.D), lambda t, nt: (t, 0)),
        scratch_shapes=[pltpu.VMEM((tm, dm.D), F32), pltpu.SemaphoreType.DMA(())])
    est = tm * dm.D * 4 + 2 * tm * dm.D * 2 + 2 * tm * dm.D * 4
    return pl.pallas_call(
        functools.partial(_gather_kernel, tm=tm), grid_spec=grid_spec,
        out_shape=jax.ShapeDtypeStruct((nt_max * tm, dm.D), BF16),
        compiler_params=_params(("arbitrary",), est),
        name="moe_gather",
    )(n_tiles, src_tok.reshape(nt_max, 1, tm), hf)


def _combine_kernel(p1_ref, p2_ref, y_hbm, g1_ref, g2_ref, x_ref, gate_ref, gpost_ref, o_ref, buf, sem, *, tm):
    def issue(r, carry):
        _row_copy(y_hbm, p1_ref[0, r], buf.at[0], r, sem).start()
        _row_copy(y_hbm, p2_ref[0, r], buf.at[1], r, sem).start()
        return carry

    lax.fori_loop(0, tm, issue, 0)
    for k in range(2):
        pltpu.make_async_copy(y_hbm.at[pl.ds(0, tm), :], buf.at[k], sem).wait()
    f = g1_ref[...] * buf[0] + g2_ref[...] * buf[1]
    o_ref[...] = x_ref[...] + gate_ref[...] * _rms(f, gpost_ref[...])


def _combine_call(dm, y_sorted, pos1, pos2, g1, g2, x, mod3, gate_lk, g_post, tm):
    d = dm.D
    nblk = dm.NL // tm
    layer, k = gate_lk
    idx_spec = pl.BlockSpec((None, 1, tm), lambda i: (i, 0, 0), memory_space=pltpu.SMEM)
    col_spec = pl.BlockSpec((tm, 1), lambda i: (i, 0))
    est = 2 * tm * d * 4 + 4 * tm * d * 4 + 4 * tm * d * 4 + 4 * tm * LANES * 4
    return pl.pallas_call(
        functools.partial(_combine_kernel, tm=tm),
        grid=(nblk,),
        in_specs=[idx_spec, idx_spec, pl.BlockSpec(memory_space=pl.ANY), col_spec, col_spec,
                  pl.BlockSpec((tm, d), lambda i: (i, 0)),
                  pl.BlockSpec((None, 1, d),
                               lambda i: ((layer * MOD_ROWS + (i * tm) // dm.S) * N_MOD + k, 0, 0)),
                  pl.BlockSpec((1, d), lambda i: (0, 0))],
        out_specs=pl.BlockSpec((tm, d), lambda i: (i, 0)),
        out_shape=jax.ShapeDtypeStruct((dm.NL, d), F32),
        scratch_shapes=[pltpu.VMEM((2, tm, d), F32), pltpu.SemaphoreType.DMA(())],
        compiler_params=_params(("arbitrary",), est),
        name="moe_combine_residual",
    )(pos1.reshape(nblk, 1, tm), pos2.reshape(nblk, 1, tm), y_sorted, g1.reshape(dm.NL, 1),
      g2.reshape(dm.NL, 1), x, mod3, g_post.reshape(1, d))


def _moe_plan(dm, rout, tm):
    t = dm.NL
    e_flat = jnp.concatenate([rout[:, 0], rout[:, 1]]).astype(jnp.int32)
    onehot = (e_flat[:, None] == jnp.arange(dm.E, dtype=jnp.int32)[None, :]).astype(jnp.int32)
    before = jnp.cumsum(onehot, axis=0) - onehot
    rank = jnp.sum(before * onehot, axis=1)
    counts = jnp.sum(onehot, axis=0)
    tiles = (counts + tm - 1) // tm
    tile_end = jnp.cumsum(tiles)
    tile_start = tile_end - tiles
    dest = jnp.sum(onehot * tile_start[None, :], axis=1) * tm + rank
    nt_max = (2 * t) // tm + dm.E
    n_tiles = tile_end[-1:].astype(jnp.int32)
    tile_ids = jnp.arange(nt_max, dtype=jnp.int32)
    tile_expert = jnp.minimum(jnp.sum((tile_ids[:, None] >= tile_end[None, :]).astype(jnp.int32), axis=1),
                              dm.E - 1).astype(jnp.int32)
    tok = jnp.concatenate([jnp.arange(t, dtype=jnp.int32)] * 2)
    src_tok = jnp.zeros((nt_max * tm,), jnp.int32).at[dest].set(tok)
    return src_tok, dest[:t], dest[t:], tile_expert, n_tiles, nt_max


def _rope_tables(dm):
    half = HEAD_DIM // 2
    freqs = ROPE_THETA ** (-jnp.arange(0, half, 2, dtype=F32) / half)
    pos = jnp.arange(dm.S, dtype=jnp.int32)
    row = (pos // dm.GW).astype(F32)[:, None] * freqs
    col = (pos % dm.GW).astype(F32)[:, None] * freqs
    cos = jnp.concatenate([jnp.cos(row), jnp.cos(row), jnp.cos(col), jnp.cos(col)], axis=1)
    sin = jnp.concatenate([-jnp.sin(row), jnp.sin(row), -jnp.sin(col), jnp.sin(col)], axis=1)
    cos = jnp.concatenate([jnp.tile(cos, (dm.B, 1)), jnp.ones((dm.NC, HEAD_DIM), F32)], axis=0)
    sin = jnp.concatenate([jnp.tile(sin, (dm.B, 1)), jnp.zeros((dm.NC, HEAD_DIM), F32)], axis=0)
    return cos, sin


def _trunk(dm, x, c, ctx, c_ctx, ada_w, ada_b, norm_mix_pre, norm_mix_post, norm_ffn_pre, norm_ffn_post,
           w_in, attn_sink, attn_norm, conv_w, conv_b, dt_bias, a_log, d_skip, ssm_norm, w_out,
           ffn_w_gate, ffn_w_up, ffn_w_down, moe_router, moe_w_gate, moe_w_up, moe_w_down):
    depth = ada_w.shape[0]
    d = dm.D
    x2 = x.reshape(dm.NL, d)
    c2 = ctx.reshape(dm.NC, d)
    cvec = jnp.concatenate([c, c_ctx[None, :], jnp.zeros((MOD_ROWS - dm.B - 1, d), F32)], axis=0)
    mod3 = _mod_call(cvec, ada_w, ada_b).reshape(depth * MOD_ROWS * N_MOD, 1, d)
    cos, sin = _rope_tables(dm)

    tm_all = _row_tile(dm.M, MM_ROW_TILE_CAP)
    tm_lat = _row_tile(dm.NL, MM_ROW_TILE_CAP)
    tn_in = _pick(math.gcd(math.gcd(dm.AD, dm.SD), math.gcd(dm.KD, dm.CD)), (512, 256, 128))
    nq, nz, nk, nx = dm.AD // tn_in, dm.SD // tn_in, dm.KD // tn_in, dm.CD // tn_in
    src_qk = lambda j: jnp.where(j < nq, j, j + nz)
    src_zxv = lambda j: jnp.where(j < nz, j + nq,
                                  jnp.where(j < nz + nx, j + nq + 2 * nk, j - nx + nq + nk))
    w_in_t = jnp.swapaxes(w_in, 1, 2)
    w_dt = w_in_t[:, dm.DT0:, :].reshape(depth, N_DIR, dm.SH, d)
    w_dt = jnp.pad(w_dt, ((0, 0), (0, 0), (0, LANES - dm.SH), (0, 0))).reshape(depth, N_DIR * LANES, d)

    xa = None
    for i in range(depth):
        last = i == depth - 1
        if i == 0:
            _, h = _norm_call(dm, dm.M, x2, c2, None, None, None, norm_mix_pre[i], (i, 0), (i, 1), mod3, BF16)
        qk = _mm_call([h], [w_in_t], i, rows=dm.M, tm=tm_all, tn=tn_in, n_cols=dm.AD + dm.KD, out_dtype=BF16,
                      rope=(cos, sin), w_nt=True, w_buffers=1, panel_src=src_qk, name="proj_in_qk")
        big = _mm_call([h], [w_in_t], i, rows=dm.M, tm=tm_all, tn=tn_in, n_cols=dm.SD + dm.CD + dm.KD,
                       out_dtype=BF16, w_nt=True, w_buffers=1, panel_src=src_zxv, name="proj_in_zxv")
        dt_pad = _mm_call([h], [w_dt], i, rows=dm.M, tm=tm_all, tn=N_DIR * LANES, n_cols=N_DIR * LANES,
                          out_dtype=F32, w_nt=True, name="proj_dt")
        u = _conv_call(dm, big, conv_w[i], conv_b[i])
        y2 = _ssd_call(dm, u, dt_pad, dt_bias[i], a_log[i], d_skip[i])
        rows = dm.NL if last else dm.M
        tm_rows = tm_lat if last else tm_all
        ssm = _gate_call(dm, rows, y2, big, ssm_norm[i])
        attn = _attn_call(dm, qk, big, attn_sink[i], attn_norm[i], with_ctx=not last)
        y = _mm_call([attn, ssm], [w_out], i, rows=rows, tm=tm_rows, tn=_pick(d, (512, 256, 128)), n_cols=d,
                     out_dtype=BF16, w_buffers=1, name="proj_out")
        if i == 0:
            xa, hf = _norm_call(dm, rows, x2, c2, y, norm_mix_post[i], (i, 2), norm_ffn_pre[i], (i, 3), (i, 4),
                                mod3, BF16 if i % 2 == 0 else F32)
        else:
            xa, hf = _norm_call(dm, rows, xa, None, y, norm_mix_post[i], (i, 2), norm_ffn_pre[i], (i, 3), (i, 4),
                                mod3, BF16 if i % 2 == 0 else F32)
        j = i // 2
        if i % 2 == 0:
            act = _mm_call([hf], [ffn_w_gate, ffn_w_up], j, rows=rows, tm=tm_rows,
                           tn=_pick(dm.DFF, (256, 128)), n_cols=dm.DFF, out_dtype=BF16, mode="swiglu",
                           w_buffers=1, name="ffn_gate_up")
            f = _mm_call([act], [ffn_w_down], j, rows=rows, tm=_row_tile(rows, MM_ROW_TILE_CAP // 2),
                         tn=_pick(d, (256, 128)), n_cols=d, out_dtype=BF16, w_buffers=1, name="ffn_down")
            if last:
                raise NotImplementedError("dense FFN on the last layer")
            xa, h = _norm_call(dm, rows, xa, None, f, norm_ffn_post[i], (i, 5), norm_mix_pre[i + 1],
                               (i + 1, 0), (i + 1, 1), mod3, BF16)
        else:
            if not last:
                raise NotImplementedError("MoE FFN on a non-final layer")
            tm_e = _pick(dm.NL, (512, 256, 128))
            tg = _pick(tm_e, (256, 128))
            rout = _router_call(dm, hf, moe_router[j])
            src_tok, pos1, pos2, tile_expert, n_tiles, nt_max = _moe_plan(dm, rout, tm_e)
            xs = _gather_call(dm, hf, src_tok, n_tiles * (tm_e // tg), tg, nt_max * (tm_e // tg))
            n_exp = moe_w_gate.shape[1]
            wg = moe_w_gate.reshape((-1,) + moe_w_gate.shape[2:])
            wu = moe_w_up.reshape((-1,) + moe_w_up.shape[2:])
            wd = moe_w_down.reshape((-1,) + moe_w_down.shape[2:])
            act = _mm_call([xs], [wg, wu], j * n_exp, rows=nt_max * tm_e, tm=tm_e,
                           tn=_pick(dm.DE, (256, 128)), n_cols=dm.DE, out_dtype=BF16, mode="swiglu",
                           grouped=(tile_expert, n_tiles), name="moe_gate_up")
            ys = _mm_call([act], [wd], j * n_exp, rows=nt_max * tm_e, tm=tm_e,
                          tn=_pick(d, (512, 256, 128)), n_cols=d, out_dtype=F32,
                          grouped=(tile_expert, n_tiles), name="moe_down")
            xa = _combine_call(dm, ys, pos1, pos2, rout[:, 2], rout[:, 3], xa, mod3, (i, 5), norm_ffn_post[i],
                               tg)
    return xa[:dm.NL].reshape(dm.B, dm.S, d)


def kernel(x, c, ctx, c_ctx, ada_w, ada_b, norm_mix_pre, norm_mix_post, norm_ffn_pre, norm_ffn_post, w_in,
           attn_sink, attn_norm, conv_w, conv_b, dt_bias, a_log, d_skip, ssm_norm, w_out, ffn_w_gate, ffn_w_up,
           ffn_w_down, moe_router, moe_w_gate, moe_w_up, moe_w_down):
    b, s, d = x.shape
    sh = a_log.shape[2]
    cd = conv_w.shape[2]
    sd = sh * SSM_HEAD_DIM
    dm = Dims(B=b, S=s, L=ctx.shape[1], D=d, AH=attn_sink.shape[1], KVH=ATTN_KV_HEADS, SH=sh,
              SG=(cd - sd) // (2 * SSM_STATE), DFF=ffn_w_gate.shape[2], E=moe_router.shape[2],
              DE=moe_w_gate.shape[3], GW=GRID_W)
    assert w_in.shape[2] == dm.DT0 + N_DIR * dm.SH and w_out.shape[1] == dm.AD + dm.SD and dm.AD == dm.SD
    assert dm.S % BLOCK == 0 and dm.L % BLOCK == 0 and dm.SH <= LANES and dm.E <= LANES
    return _trunk(dm, x, c, ctx, c_ctx, ada_w, ada_b, norm_mix_pre, norm_mix_post, norm_ffn_pre, norm_ffn_post,
                  w_in, attn_sink, attn_norm, conv_w, conv_b, dt_bias, a_log, d_skip, ssm_norm, w_out,
                  ffn_w_gate, ffn_w_up, ffn_w_down, moe_router, moe_w_gate, moe_w_up, moe_w_down)
```

```python
import functools
import math
from typing import NamedTuple

import jax
import jax.numpy as jnp
from jax import lax
from jax.experimental import pallas as pl
from jax.experimental.pallas import tpu as pltpu

F32 = jnp.float32
BF16 = jnp.bfloat16

HEAD_DIM = 128
WINDOW = 128
BLOCK = 128
ROPE_THETA = 10000.0
SSM_HEAD_DIM = 64
SSM_STATE = 128
CONV_K = 5
CHUNK = 128
N_DIR = 2
GRID_W = 64
ATTN_KV_HEADS = 4
N_MOD = 6
EPS = 1e-6
MOD_ROWS = 8

V7X_VMEM_BYTES = 64 * 1024 * 1024
VMEM_LIMIT_CAP = 56 * 1024 * 1024
LANES = 128
BF16_SUBLANES = 16
MM_ROW_TILE_CAP = 1100


class Dims(NamedTuple):
    B: int
    S: int
    L: int
    D: int
    AH: int
    KVH: int
    SH: int
    SG: int
    DFF: int
    E: int
    DE: int
    GW: int

    @property
    def AD(self):
        return self.AH * HEAD_DIM

    @property
    def KD(self):
        return self.KVH * HEAD_DIM

    @property
    def SD(self):
        return self.SH * SSM_HEAD_DIM

    @property
    def BC(self):
        return self.SG * SSM_STATE

    @property
    def CD(self):
        return self.SD + 2 * self.BC

    @property
    def K0(self):
        return self.AD + self.SD

    @property
    def V0(self):
        return self.K0 + self.KD

    @property
    def X0(self):
        return self.V0 + self.KD

    @property
    def DT0(self):
        return self.X0 + self.CD

    @property
    def NL(self):
        return self.B * self.S

    @property
    def NC(self):
        return self.B * self.L

    @property
    def M(self):
        return self.NL + self.NC


def _pick(n, prefs):
    for p in prefs:
        if n % p == 0:
            return p
    raise ValueError(f"no tile in {prefs} divides {n}")


def _row_tile(rows, cap):
    for t in range(min(cap, rows), 0, -1):
        if rows % t == 0 and t % BF16_SUBLANES == 0:
            return t
    raise ValueError(f"no row tile <= {cap} divides {rows}")


def _params(sem, est_bytes):
    limit = int(min(max(est_bytes * 5 // 4 + (2 << 20), 16 << 20), VMEM_LIMIT_CAP))
    return pltpu.CompilerParams(dimension_semantics=sem, vmem_limit_bytes=limit)


def _silu(v):
    return v / (1.0 + jnp.exp(-v))


def _rms(v, g):
    return v * lax.rsqrt(jnp.mean(v * v, axis=-1, keepdims=True) + EPS) * g


def _mod_kernel(c_ref, w_ref, b_ref, o_ref):
    c = _silu(c_ref[...]).astype(BF16)
    o_ref[...] = jnp.dot(c, w_ref[...].astype(BF16), preferred_element_type=F32) + b_ref[...]


def _mod_call(cvec, ada_w, ada_b):
    depth, d, n = ada_w.shape
    tn = _pick(n, (512, 256, 128))
    est = 2 * d * tn * 4 + d * tn * 2 + 4 * MOD_ROWS * tn * 4
    return pl.pallas_call(
        _mod_kernel,
        grid=(depth, n // tn),
        in_specs=[
            pl.BlockSpec((MOD_ROWS, d), lambda l, j: (0, 0)),
            pl.BlockSpec((None, d, tn), lambda l, j: (l, 0, j)),
            pl.BlockSpec((None, 1, tn), lambda l, j: (l, 0, j)),
        ],
        out_specs=pl.BlockSpec((None, MOD_ROWS, tn), lambda l, j: (l, 0, j)),
        out_shape=jax.ShapeDtypeStruct((depth, MOD_ROWS, n), F32),
        compiler_params=_params(("arbitrary", "arbitrary"), est),
        name="adaln_mod",
    )(cvec, ada_w, ada_b.reshape(depth, 1, n))


def _norm_kernel(*refs, split, has_res, n_lat_blocks):
    refs = list(refs)
    x_ref = refs.pop(0)
    c_ref = refs.pop(0) if split else None
    if has_res:
        y_ref, gpost_ref, gate_ref = refs.pop(0), refs.pop(0), refs.pop(0)
    gpre_ref, sh_ref, sc_ref = refs.pop(0), refs.pop(0), refs.pop(0)
    if has_res:
        xo_ref = refs.pop(0)
    h_ref = refs.pop(0)

    def run(src_ref):
        x = src_ref[...]
        if has_res:
            x = x + gate_ref[...] * _rms(y_ref[...].astype(F32), gpost_ref[...])
            xo_ref[...] = x
        h_ref[...] = (_rms(x, gpre_ref[...]) * (1.0 + sc_ref[...]) + sh_ref[...]).astype(h_ref.dtype)

    if split:
        i = pl.program_id(0)
        pl.when(i < n_lat_blocks)(lambda: run(x_ref))
        pl.when(i >= n_lat_blocks)(lambda: run(c_ref))
    else:
        run(x_ref)


def _norm_call(dm, rows, x, ctx2d, y, g_post, gate_lk, g_pre, sh_lk, sc_lk, mod3, h_dtype):
    d = dm.D
    tm = _pick(dm.L, (256, 128))
    nblk = rows // tm
    nlat = dm.NL // tm
    split = ctx2d is not None
    has_res = y is not None

    def mod_spec(lk):
        layer, k = lk
        return pl.BlockSpec(
            (None, 1, d),
            lambda i: ((layer * MOD_ROWS + jnp.minimum((i * tm) // dm.S, dm.B)) * N_MOD + k, 0, 0))

    row_spec = pl.BlockSpec((tm, d), lambda i: (i, 0))
    vec_spec = pl.BlockSpec((1, d), lambda i: (0, 0))
    in_specs, args = [], []
    if split:
        in_specs += [pl.BlockSpec((tm, d), lambda i: (jnp.minimum(i, nlat - 1), 0)),
                     pl.BlockSpec((tm, d), lambda i: (jnp.maximum(i - nlat, 0), 0))]
        args += [x, ctx2d]
    else:
        in_specs += [row_spec]
        args += [x]
    if has_res:
        in_specs += [row_spec, vec_spec, mod_spec(gate_lk)]
        args += [y, g_post.reshape(1, d), mod3]
    in_specs += [vec_spec, mod_spec(sh_lk), mod_spec(sc_lk)]
    args += [g_pre.reshape(1, d), mod3, mod3]
    out_specs, out_shape = [], []
    if has_res:
        out_specs.append(row_spec)
        out_shape.append(jax.ShapeDtypeStruct((rows, d), F32))
    out_specs.append(row_spec)
    out_shape.append(jax.ShapeDtypeStruct((rows, d), h_dtype))
    est = 2 * tm * d * (4 + 4 + 2 + 4 + 4)
    outs = pl.pallas_call(
        functools.partial(_norm_kernel, split=split, has_res=has_res, n_lat_blocks=nlat),
        grid=(nblk,),
        in_specs=in_specs,
        out_specs=out_specs,
        out_shape=out_shape,
        compiler_params=_params(("arbitrary",), est),
        name="residual_prenorm" if has_res else "prenorm",
    )(*args)
    return outs if has_res else (None, outs[0])


def _rope(acc, cos, sin_signed, tn):
    lane = lax.broadcasted_iota(jnp.int32, acc.shape, 1)
    first = (lane % 64) < 32
    partner = jnp.where(first, pltpu.roll(acc, tn - 32, 1), pltpu.roll(acc, 32, 1))
    reps = tn // HEAD_DIM
    cos_t = jnp.concatenate([cos] * reps, axis=1) if reps > 1 else cos
    sin_t = jnp.concatenate([sin_signed] * reps, axis=1) if reps > 1 else sin_signed
    return acc * cos_t + partner * sin_t


def _mm_kernel(*refs, n_lhs, k_sizes, n_w, mode, with_rope, grouped, tn, w_nt, w_group, n_panels, panel_src):
    refs = list(refs)
    if grouped:
        te_ref, nxt_ref, nt_ref = refs.pop(0), refs.pop(0), refs.pop(0)
    lhs_refs = [refs.pop(0) for _ in range(n_lhs)]
    w_refs = [refs.pop(0) for _ in range(n_w)]
    if with_rope:
        cos_ref, sin_ref = refs.pop(0), refs.pop(0)
    o_ref = refs.pop(0)
    wb_refs = [refs.pop(0) for _ in range(n_w)]
    stage_refs = [refs.pop(0) for _ in range(n_w)]
    sem = refs.pop(0)
    j = pl.program_id(0)
    m = pl.program_id(1)

    def panel_copy(i, g, jp):
        col = pl.multiple_of(panel_src(jp) * tn, tn)
        src = w_refs[i].at[g, pl.ds(col, tn), :] if w_nt else w_refs[i].at[g, :, pl.ds(col, tn)]
        return pltpu.make_async_copy(src, stage_refs[i], sem.at[i])

    if grouped:
        nt = nt_ref[0]
        active = m < nt
        new_panel = jnp.logical_or(m == 0, jnp.logical_and(active, te_ref[m] != te_ref[jnp.maximum(m - 1, 0)]))
        g_cur = w_group + te_ref[m]
        more_here = nxt_ref[m] >= 0
        g_next = w_group + jnp.where(more_here, nxt_ref[m], te_ref[0])
        j_next = jnp.where(more_here, j, j + 1)
        has_next = jnp.logical_or(more_here, j + 1 < n_panels)
    else:
        active = None
        new_panel = m == 0
        g_cur = g_next = w_group
        j_next = j + 1
        has_next = j + 1 < n_panels

    @pl.when(jnp.logical_and(j == 0, m == 0))
    def _():
        for i in range(n_w):
            panel_copy(i, g_cur, j).start()

    @pl.when(new_panel)
    def _():
        for i in range(n_w):
            panel_copy(i, g_cur, j).wait()
            wb_refs[i][...] = stage_refs[i][...].astype(BF16)

        @pl.when(has_next)
        def _():
            for i in range(n_w):
                panel_copy(i, g_next, j_next).start()

    def compute():
        accs = []
        for wb_ref in wb_refs:
            acc, k0 = None, 0
            for l_ref, ks in zip(lhs_refs, k_sizes):
                if w_nt:
                    part = lax.dot_general(l_ref[...], wb_ref[:, k0:k0 + ks], (((1,), (1,)), ((), ())),
                                           preferred_element_type=F32)
                else:
                    part = jnp.dot(l_ref[...], wb_ref[k0:k0 + ks, :], preferred_element_type=F32)
                acc = part if acc is None else acc + part
                k0 += ks
            accs.append(acc)
        res = _silu(accs[0]) * accs[1] if mode == "swiglu" else accs[0]
        if with_rope:
            res = _rope(res, cos_ref[...], sin_ref[...], tn)
        o_ref[...] = res.astype(o_ref.dtype)

    if grouped:
        pl.when(active)(compute)

        @pl.when(jnp.logical_not(active))
        def _():
            o_ref[...] = jnp.zeros_like(o_ref)
    else:
        compute()


def _mm_call(lhs_list, w_list, w_group, *, rows, tm, tn, n_cols, out_dtype, mode="plain",
             rope=None, grouped=None, w_nt=False, panel_src=None, name="panel_matmul"):
    k_sizes = tuple(a.shape[1] for a in lhs_list)
    k_tot = sum(k_sizes)
    n_w = len(w_list)
    assert all(w.shape[2 if w_nt else 1] == k_tot for w in w_list)
    n_panels = n_cols // tn
    n_blocks = rows // tm
    args, in_specs = [], []
    src = panel_src if panel_src is not None else (lambda j: j)
    if grouped is not None:
        lhs_map = lambda j, m, te, nx, nt: (jnp.minimum(m, nt[0] - 1), 0)
        out_map = lambda j, m, te, nx, nt: (m, j)
    else:
        lhs_map = lambda j, m: (m, 0)
        out_map = lambda j, m: (m, j)
    for a, ks in zip(lhs_list, k_sizes):
        in_specs.append(pl.BlockSpec((tm, ks), lhs_map))
        args.append(a)
    w_block = (tn, k_tot) if w_nt else (k_tot, tn)
    for w in w_list:
        in_specs.append(pl.BlockSpec(memory_space=pl.ANY))
        args.append(w)
    if rope is not None:
        cos, sin_signed = rope
        in_specs += [pl.BlockSpec((tm, HEAD_DIM), lambda j, m: (m, 0))] * 2
        args += [cos, sin_signed]
    out_bytes = jnp.dtype(out_dtype).itemsize
    est = (n_w * k_tot * tn * (4 + 2) + 2 * tm * k_tot * 2 + 2 * tm * tn * out_bytes + (2 + n_w) * tm * tn * 4)
    kern = functools.partial(_mm_kernel, n_lhs=len(lhs_list), k_sizes=k_sizes, n_w=n_w, mode=mode,
                             with_rope=rope is not None, grouped=grouped is not None, tn=tn, w_nt=w_nt,
                             w_group=w_group, n_panels=n_panels, panel_src=src)
    scratch = ([pltpu.VMEM(w_block, BF16) for _ in range(n_w)] + [pltpu.VMEM(w_block, F32) for _ in range(n_w)]
               + [pltpu.SemaphoreType.DMA((n_w,))])
    out_shape = jax.ShapeDtypeStruct((rows, n_cols), out_dtype)
    cp = _params(("arbitrary", "arbitrary"), est)
    if grouped is not None:
        grid_spec = pltpu.PrefetchScalarGridSpec(
            num_scalar_prefetch=3, grid=(n_panels, n_blocks), in_specs=in_specs,
            out_specs=pl.BlockSpec((tm, tn), out_map), scratch_shapes=scratch)
        return pl.pallas_call(kern, grid_spec=grid_spec, out_shape=out_shape, compiler_params=cp,
                              name=name)(*grouped, *args)
    return pl.pallas_call(kern, grid=(n_panels, n_blocks), in_specs=in_specs,
                          out_specs=pl.BlockSpec((tm, tn), out_map), out_shape=out_shape,
                          scratch_shapes=scratch, compiler_params=cp, name=name)(*args)


def _conv_kernel(prev_ref, cur_ref, next_ref, w_ref, b_ref, o_ref, *, tb, seg_lat, seg_ctx, n_lat_rows):
    i = pl.program_id(0)
    row0 = i * tb
    in_lat = row0 < n_lat_rows
    off = jnp.where(in_lat, row0 % seg_lat, (row0 - n_lat_rows) % seg_ctx)
    seg = jnp.where(in_lat, seg_lat, seg_ctx)
    keep_prev = jnp.where(off != 0, 1.0, 0.0)
    keep_next = jnp.where(off + tb != seg, 1.0, 0.0)
    halo = BF16_SUBLANES
    prev = prev_ref[...].astype(F32)[halo - 8:, :] * keep_prev
    nxt = next_ref[...].astype(F32)[:8, :] * keep_next
    ext = jnp.concatenate([prev, cur_ref[...].astype(F32), nxt], axis=0)
    rows = tb + 16
    pad = CONV_K // 2
    acc = None
    for k in range(CONV_K):
        shift = (pad - k) % rows
        shifted = ext if shift == 0 else pltpu.roll(ext, shift, 0)
        term = shifted[8:8 + tb, :] * w_ref[k:k + 1, :]
        acc = term if acc is None else acc + term
    o_ref[...] = _silu(acc + b_ref[...]).astype(o_ref.dtype)


def _conv_call(dm, big, conv_w, conv_b):
    tb = _pick(dm.L, (256, 128))
    cw = _pick(math.gcd(dm.SD, dm.CD), (1024, 512, 256, 128))
    halo = BF16_SUBLANES
    col0 = dm.SD // cw
    nrb = dm.M // tb
    last_halo = dm.M // halo - 1
    est = 2 * (tb + 2 * halo) * cw * 2 + 2 * tb * cw * 2 + 8 * (tb + 16) * cw * 4
    return pl.pallas_call(
        functools.partial(_conv_kernel, tb=tb, seg_lat=dm.S, seg_ctx=dm.L, n_lat_rows=dm.NL),
        grid=(nrb, dm.CD // cw),
        in_specs=[
            pl.BlockSpec((halo, cw), lambda i, j: (jnp.maximum(i * (tb // halo) - 1, 0), col0 + j)),
            pl.BlockSpec((tb, cw), lambda i, j: (i, col0 + j)),
            pl.BlockSpec((halo, cw), lambda i, j: (jnp.minimum((i + 1) * (tb // halo), last_halo), col0 + j)),
            pl.BlockSpec((8, cw), lambda i, j: (0, j)),
            pl.BlockSpec((1, cw), lambda i, j: (0, j)),
        ],
        out_specs=pl.BlockSpec((tb, cw), lambda i, j: (i, j)),
        out_shape=jax.ShapeDtypeStruct((dm.M, dm.CD), BF16),
        compiler_params=_params(("arbitrary", "arbitrary"), est),
        name="dwconv_silu",
    )(big, big, big, jnp.pad(conv_w, ((0, 8 - CONV_K), (0, 0))), conv_b.reshape(1, dm.CD))


def _split3(v):
    hi = v.astype(BF16)
    r1 = v - hi.astype(F32)
    mid = r1.astype(BF16)
    lo = (r1 - mid.astype(F32)).astype(BF16)
    return hi, mid, lo


def _softplus(v):
    t = jnp.exp(-jnp.abs(v))
    y = 1.0 + t
    return jnp.maximum(v, 0.0) + (jnp.log(y) - ((y - 1.0) - t) / y)


def _ssd_kernel(u_ref, dt_ref, bias_ref, alog_ref, dsk_ref, y_ref, st_ref, *, SH, SG):
    d = pl.program_id(1)
    s = pl.program_id(2)
    hpg = SH // SG
    sd = SH * SSM_HEAD_DIM
    bc = SG * SSM_STATE
    c = CHUNK

    @pl.when(s == 0)
    def _():
        st_ref[...] = jnp.zeros_like(st_ref)

    dt = _softplus(dt_ref[...] + bias_ref[...])
    a = dt * (-jnp.exp(alog_ref[...]))
    row = lax.broadcasted_iota(jnp.int32, (c, c), 0)
    col = lax.broadcasted_iota(jnp.int32, (c, c), 1)
    fwd = d == 0
    tri = jnp.where(fwd, col, row) <= jnp.where(fwd, row, col)
    hi, mid, lo = _split3(a)
    tri_b = jnp.where(tri, 1.0, 0.0).astype(BF16)
    parts = jnp.dot(tri_b, jnp.concatenate([hi, mid, lo], axis=1), preferred_element_type=F32)
    cum = parts[:, :LANES] + parts[:, LANES:2 * LANES] + parts[:, 2 * LANES:]
    tot = jnp.sum(a, axis=0, keepdims=True)
    w_t = (jnp.exp(tot - cum) * dt).T
    cum_t = cum.T
    dt_t = dt.T
    etot = jnp.exp(tot)
    lane_lo = lax.broadcasted_iota(jnp.int32, (c, LANES), 1) < SSM_HEAD_DIM
    lane_lo1 = lax.broadcasted_iota(jnp.int32, (1, LANES), 1) < SSM_HEAD_DIM
    neg_inf = jnp.float32(-jnp.inf)

    for g in range(SG):
        b_g = u_ref[:, sd + g * SSM_STATE: sd + (g + 1) * SSM_STATE]
        c_g = u_ref[:, sd + bc + g * SSM_STATE: sd + bc + (g + 1) * SSM_STATE]
        cb = lax.dot_general(c_g, b_g, (((1,), (1,)), ((), ())), preferred_element_type=F32)
        bt_g = b_g.astype(F32).T
        c_g32 = c_g.astype(F32)
        for pr in range(hpg // 2):
            h0 = g * hpg + 2 * pr
            lo_col, hi_col = h0 * SSM_HEAD_DIM, (h0 + 2) * SSM_HEAD_DIM
            xs_pair = u_ref[:, lo_col:hi_col]
            st_pair = st_ref[:, lo_col:hi_col]
            rhs = jnp.concatenate([xs_pair, st_pair.astype(BF16)], axis=0)
            ys, sts, ets = [], [], []
            for q in range(2):
                h = h0 + q
                colb = jnp.broadcast_to(cum[:, h:h + 1], (c, c))
                decay = jnp.exp(jnp.where(tri, colb - cum_t[h:h + 1, :], neg_inf))
                m_h = decay * cb * dt_t[h:h + 1, :]
                c_h = c_g32 * jnp.exp(colb)
                lhs = jnp.concatenate([m_h, c_h], axis=1).astype(BF16)
                ys.append(jnp.dot(lhs, rhs, preferred_element_type=F32))
                btw = (bt_g * w_t[h:h + 1, :]).astype(BF16)
                sts.append(jnp.dot(btw, xs_pair, preferred_element_type=F32))
                ets.append(jnp.broadcast_to(etot[:, h:h + 1], (1, LANES)))
            y_pair = jnp.where(lane_lo, ys[0], ys[1]) + dsk_ref[:, lo_col:hi_col] * xs_pair.astype(F32)
            e_pair = jnp.where(lane_lo1, ets[0], ets[1])
            st_ref[:, lo_col:hi_col] = e_pair * st_pair + jnp.where(lane_lo, sts[0], sts[1])
            y_ref[:, lo_col:hi_col] = y_pair.astype(y_ref.dtype)


def _ssd_call(dm, u, dt_pad, dt_bias, a_log, d_skip):
    nl, ns = dm.L // CHUNK, dm.S // CHUNK
    lat_blocks = dm.NL // CHUNK

    def rb(b, d, s):
        ctx_chunk = jnp.where(d == 0, s, nl - 1 - s)
        lat_chunk = jnp.where(d == 0, s - nl, ns - 1 - (s - nl))
        return jnp.where(s < nl, lat_blocks + b * nl + ctx_chunk, b * ns + lat_chunk)

    pad = LANES - dm.SH
    bias_p = jnp.pad(dt_bias, ((0, 0), (0, pad))).reshape(N_DIR, 1, LANES)
    alog_p = jnp.pad(a_log, ((0, 0), (0, pad))).reshape(N_DIR, 1, LANES)
    dsk = jnp.repeat(d_skip, SSM_HEAD_DIM)
    dsk_p = jnp.stack([dsk, jnp.zeros_like(dsk)]).reshape(N_DIR, 1, dm.SD)
    est = 2 * CHUNK * dm.CD * 2 + 2 * CHUNK * dm.SD * 2 + SSM_STATE * dm.SD * 4 + (8 << 20)
    return pl.pallas_call(
        functools.partial(_ssd_kernel, SH=dm.SH, SG=dm.SG),
        grid=(dm.B, N_DIR, nl + ns),
        in_specs=[
            pl.BlockSpec((CHUNK, dm.CD), lambda b, d, s: (rb(b, d, s), 0)),
            pl.BlockSpec((CHUNK, LANES), lambda b, d, s: (rb(b, d, s), d)),
            pl.BlockSpec((None, 1, LANES), lambda b, d, s: (d, 0, 0)),
            pl.BlockSpec((None, 1, LANES), lambda b, d, s: (d, 0, 0)),
            pl.BlockSpec((None, 1, dm.SD), lambda b, d, s: (d, 0, 0)),
        ],
        out_specs=pl.BlockSpec((None, CHUNK, dm.SD), lambda b, d, s: (d, rb(b, d, s), 0)),
        out_shape=jax.ShapeDtypeStruct((N_DIR, dm.M, dm.SD), BF16),
        scratch_shapes=[pltpu.VMEM((SSM_STATE, dm.SD), F32)],
        compiler_params=_params(("arbitrary", "arbitrary", "arbitrary"), est),
        name="ssd_scan",
    )(u, dt_pad, bias_p, alog_p, dsk_p)


def _gate_kernel(yf_ref, yb_ref, z_ref, g_ref, o_ref):
    y = yf_ref[...].astype(F32) + yb_ref[...].astype(F32)
    o_ref[...] = _rms(y * _silu(z_ref[...].astype(F32)), g_ref[...]).astype(o_ref.dtype)


def _gate_call(dm, rows, y2, big, ssm_norm):
    tm = _pick(dm.L, (256, 128))
    sd = dm.SD
    zc = 0
    est = 2 * tm * sd * 2 * 4 + 6 * tm * sd * 4
    return pl.pallas_call(
        _gate_kernel,
        grid=(rows // tm,),
        in_specs=[
            pl.BlockSpec((None, tm, sd), lambda i: (0, i, 0)),
            pl.BlockSpec((None, tm, sd), lambda i: (1, i, 0)),
            pl.BlockSpec((tm, sd), lambda i: (i, zc)),
            pl.BlockSpec((1, sd), lambda i: (0, 0)),
        ],
        out_specs=pl.BlockSpec((tm, sd), lambda i: (i, 0)),
        out_shape=jax.ShapeDtypeStruct((rows, sd), BF16),
        compiler_params=_params(("arbitrary",), est),
        name="ssd_gate_norm",
    )(y2, y2, big, ssm_norm.reshape(1, sd))


def _attn_kernel(sink_ref, q_ref, kp_ref, kc_ref, kn_ref, vp_ref, vc_ref, vn_ref, kx_ref, vx_ref, g_ref,
                 o_ref, *, KVH, G, n_lat, L):
    i = pl.program_id(1)
    is_ctx = i >= n_lat
    band = 3 * BLOCK
    width = band + L
    rowq = lax.broadcasted_iota(jnp.int32, (BLOCK, width), 0)
    colk = lax.broadcasted_iota(jnp.int32, (BLOCK, width), 1)
    first_valid = jnp.where(i == 0, BLOCK, 0)
    last_valid = jnp.where(is_ctx, -1, jnp.where(i == n_lat - 1, 2 * BLOCK - 1, band - 1))
    lower = jnp.maximum(rowq, first_valid)
    upper = jnp.minimum(rowq + 2 * WINDOW, last_valid)
    neg_inf = jnp.float32(-jnp.inf)
    in_band = jnp.where(colk >= lower, jnp.where(colk <= upper, 0.0, neg_inf), neg_inf)
    bias = jnp.where(colk >= band, 0.0, in_band)
    scale = HEAD_DIM ** -0.5
    outs = []
    for hk in range(KVH):
        ks = slice(hk * HEAD_DIM, (hk + 1) * HEAD_DIM)
        qh = jnp.concatenate([q_ref[:, (hk * G + g) * HEAD_DIM:(hk * G + g + 1) * HEAD_DIM] for g in range(G)],
                             axis=0)
        kb = jnp.concatenate([kp_ref[:, ks], kc_ref[:, ks], kn_ref[:, ks], kx_ref[:, ks]], axis=0)
        vb = jnp.concatenate([vp_ref[:, ks], vc_ref[:, ks], vn_ref[:, ks], vx_ref[:, ks]], axis=0)
        sc = lax.dot_general(qh, kb, (((1,), (1,)), ((), ())), preferred_element_type=F32) * scale
        ps, dens = [], []
        for g in range(G):
            sg = sc[g * BLOCK:(g + 1) * BLOCK, :] + bias
            sk = sink_ref[hk * G + g]
            mx = jnp.maximum(jnp.max(sg, axis=-1, keepdims=True), sk)
            p = jnp.exp(sg - mx)
            dens.append(jnp.sum(p, axis=-1, keepdims=True) + jnp.exp(sk - mx))
            ps.append(p.astype(BF16))
        pv = jnp.dot(jnp.concatenate(ps, axis=0), vb, preferred_element_type=F32)
        for g in range(G):
            outs.append(pv[g * BLOCK:(g + 1) * BLOCK, :] / dens[g])
    o = jnp.concatenate(outs, axis=1)
    o_ref[...] = _rms(o, g_ref[...]).astype(o_ref.dtype)


def _attn_call(dm, qk, big, sink, attn_norm, with_ctx):
    ns, nl = dm.S // BLOCK, dm.L // BLOCK
    lat_blocks = dm.NL // BLOCK
    steps = ns + (nl if with_ctx else 0)
    rows = dm.M if with_ctx else dm.NL
    kd, ad = dm.KD, dm.AD
    kcol, vcol = ad // kd, (dm.SD + dm.CD) // kd

    def qrow(b, i):
        return jnp.where(i < ns, b * ns + i, lat_blocks + b * nl + (i - ns))

    def band_row(b, i, off):
        return b * ns + jnp.clip(i + off, 0, ns - 1)

    def band_spec(col, off):
        return pl.BlockSpec((BLOCK, kd), lambda b, i: (band_row(b, i, off), col))

    ctx_row0 = dm.NL // dm.L
    est = 2 * BLOCK * ad * 2 * 2 + 2 * 6 * BLOCK * kd * 2 + 2 * 2 * dm.L * kd * 2 + (12 << 20)
    return pl.pallas_call(
        functools.partial(_attn_kernel, KVH=dm.KVH, G=dm.AH // dm.KVH, n_lat=ns, L=dm.L),
        grid=(dm.B, steps),
        in_specs=[
            pl.BlockSpec(memory_space=pltpu.SMEM),
            pl.BlockSpec((BLOCK, ad), lambda b, i: (qrow(b, i), 0)),
            band_spec(kcol, -1), band_spec(kcol, 0), band_spec(kcol, 1),
            band_spec(vcol, -1), band_spec(vcol, 0), band_spec(vcol, 1),
            pl.BlockSpec((dm.L, kd), lambda b, i: (ctx_row0 + b, kcol)),
            pl.BlockSpec((dm.L, kd), lambda b, i: (ctx_row0 + b, vcol)),
            pl.BlockSpec((1, ad), lambda b, i: (0, 0)),
        ],
        out_specs=pl.BlockSpec((BLOCK, ad), lambda b, i: (qrow(b, i), 0)),
        out_shape=jax.ShapeDtypeStruct((rows, ad), BF16),
        compiler_params=_params(("arbitrary", "arbitrary"), est),
        name="band_attention",
    )(sink, qk, qk, qk, qk, big, big, big, qk, big, attn_norm.reshape(1, ad))


def _router_kernel(h_ref, r_ref, o_ref, *, E):
    x = h_ref[...]
    hi = x.astype(BF16)
    lo = (x - hi.astype(F32)).astype(BF16)
    r = r_ref[...]
    a = jnp.dot(hi, r, preferred_element_type=F32)
    b = jnp.dot(lo, r, preferred_element_type=F32)
    logits = (a[:, :LANES] + a[:, LANES:]) + (b[:, :LANES] + b[:, LANES:])
    lane = lax.broadcasted_iota(jnp.int32, logits.shape, 1)
    neg_inf = jnp.float32(-jnp.inf)
    lg = jnp.where(lane < E, logits, neg_inf)
    m1 = jnp.max(lg, axis=-1, keepdims=True)
    i1 = jnp.min(jnp.where(lg == m1, lane, LANES), axis=-1, keepdims=True)
    lg2 = jnp.where(lane == i1, neg_inf, lg)
    m2 = jnp.max(lg2, axis=-1, keepdims=True)
    i2 = jnp.min(jnp.where(lg2 == m2, lane, LANES), axis=-1, keepdims=True)
    e = jnp.exp(m2 - m1)
    g1 = 1.0 / (1.0 + e)
    g2 = e / (1.0 + e)
    o_ref[...] = jnp.where(lane == 0, i1.astype(F32),
                           jnp.where(lane == 1, i2.astype(F32),
                                     jnp.where(lane == 2, g1, jnp.where(lane == 3, g2, 0.0))))


def _router_call(dm, hf, router):
    tm = _pick(dm.NL, (512, 256, 128))
    r = jnp.pad(router, ((0, 0), (0, LANES - dm.E)))
    r_hi = r.astype(BF16)
    r_lo = (r - r_hi.astype(F32)).astype(BF16)
    r2 = jnp.concatenate([r_hi, r_lo], axis=1)
    est = 2 * tm * dm.D * 4 + 2 * dm.D * 2 * LANES * 2 + 4 * tm * dm.D * 2
    return pl.pallas_call(
        functools.partial(_router_kernel, E=dm.E),
        grid=(dm.NL // tm,),
        in_specs=[pl.BlockSpec((tm, dm.D), lambda i: (i, 0)),
                  pl.BlockSpec((dm.D, 2 * LANES), lambda i: (0, 0))],
        out_specs=pl.BlockSpec((tm, LANES), lambda i: (i, 0)),
        out_shape=jax.ShapeDtypeStruct((dm.NL, LANES), F32),
        compiler_params=_params(("arbitrary",), est),
        name="moe_router",
    )(hf, r2)


def _row_copy(src_hbm, row, dst, slot, sem):
    return pltpu.make_async_copy(src_hbm.at[pl.ds(row, 1), :], dst.at[pl.ds(slot, 1), :], sem)


ROW_COPY_UNROLL = 8


def _issue_rows(src_hbm, idx_ref, dst, sem, tm):
    def issue(r, carry):
        _row_copy(src_hbm, idx_ref[0, r], dst, r, sem).start()
        return carry

    lax.fori_loop(0, tm, issue, 0, unroll=ROW_COPY_UNROLL)


def _wait_rows(src_hbm, dst, sem, tm):
    pltpu.make_async_copy(src_hbm.at[pl.ds(0, tm), :], dst, sem).wait()


def _gather_kernel(nt_ref, src_ref, src_next_ref, h_hbm, o_ref, buf, sem, *, tm):
    t = pl.program_id(0)
    nt = nt_ref[0]
    slot = t % 2

    @pl.when(t == 0)
    def _():
        _issue_rows(h_hbm, src_ref, buf.at[0], sem.at[0], tm)

    @pl.when(t + 1 < nt)
    def _():
        _issue_rows(h_hbm, src_next_ref, buf.at[1 - slot], sem.at[1 - slot], tm)

    @pl.when(t < nt)
    def _():
        _wait_rows(h_hbm, buf.at[slot], sem.at[slot], tm)
        o_ref[...] = buf[slot].astype(o_ref.dtype)

    @pl.when(t >= nt)
    def _():
        o_ref[...] = jnp.zeros_like(o_ref)


def _gather_call(dm, hf, src_tok, n_tiles, tm, nt_max):
    src3 = src_tok.reshape(nt_max, 1, tm)
    grid_spec = pltpu.PrefetchScalarGridSpec(
        num_scalar_prefetch=1, grid=(nt_max,),
        in_specs=[pl.BlockSpec((None, 1, tm), lambda t, nt: (t, 0, 0), memory_space=pltpu.SMEM),
                  pl.BlockSpec((None, 1, tm), lambda t, nt: (jnp.minimum(t + 1, nt_max - 1), 0, 0),
                               memory_space=pltpu.SMEM),
                  pl.BlockSpec(memory_space=pl.ANY)],
        out_specs=pl.BlockSpec((tm, dm.D), lambda t, nt: (t, 0)),
        scratch_shapes=[pltpu.VMEM((2, tm, dm.D), F32), pltpu.SemaphoreType.DMA((2,))])
    est = 2 * tm * dm.D * 4 + 2 * tm * dm.D * 2 + 2 * tm * dm.D * 4
    return pl.pallas_call(
        functools.partial(_gather_kernel, tm=tm), grid_spec=grid_spec,
        out_shape=jax.ShapeDtypeStruct((nt_max * tm, dm.D), BF16),
        compiler_params=_params(("arbitrary",), est),
        name="moe_gather",
    )(n_tiles, src3, src3, hf)


def _combine_kernel(p1_ref, p2_ref, p1n_ref, p2n_ref, y_hbm, g1_ref, g2_ref, x_ref, gate_ref, gpost_ref, o_ref,
                    buf, sem, *, tm, n_blocks):
    t = pl.program_id(0)
    slot = t % 2

    def issue(first_ref, second_ref, s):
        _issue_rows(y_hbm, first_ref, buf.at[s, 0], sem.at[s], tm)
        _issue_rows(y_hbm, second_ref, buf.at[s, 1], sem.at[s], tm)

    @pl.when(t == 0)
    def _():
        issue(p1_ref, p2_ref, 0)

    @pl.when(t + 1 < n_blocks)
    def _():
        issue(p1n_ref, p2n_ref, 1 - slot)

    for k in range(2):
        _wait_rows(y_hbm, buf.at[slot, k], sem.at[slot], tm)
    f = g1_ref[...] * buf[slot, 0] + g2_ref[...] * buf[slot, 1]
    o_ref[...] = x_ref[...] + gate_ref[...] * _rms(f, gpost_ref[...])


def _combine_call(dm, y_sorted, pos1, pos2, g1, g2, x, mod3, gate_lk, g_post, tm):
    d = dm.D
    nblk = dm.NL // tm
    layer, k = gate_lk
    idx_spec = pl.BlockSpec((None, 1, tm), lambda i: (i, 0, 0), memory_space=pltpu.SMEM)
    idx_next = pl.BlockSpec((None, 1, tm), lambda i: (jnp.minimum(i + 1, nblk - 1), 0, 0),
                            memory_space=pltpu.SMEM)
    col_spec = pl.BlockSpec((tm, 1), lambda i: (i, 0))
    est = 4 * tm * d * 4 + 4 * tm * d * 4 + 4 * tm * d * 4 + 4 * tm * LANES * 4
    p1, p2 = pos1.reshape(nblk, 1, tm), pos2.reshape(nblk, 1, tm)
    return pl.pallas_call(
        functools.partial(_combine_kernel, tm=tm, n_blocks=nblk),
        grid=(nblk,),
        in_specs=[idx_spec, idx_spec, idx_next, idx_next, pl.BlockSpec(memory_space=pl.ANY), col_spec, col_spec,
                  pl.BlockSpec((tm, d), lambda i: (i, 0)),
                  pl.BlockSpec((None, 1, d),
                               lambda i: ((layer * MOD_ROWS + (i * tm) // dm.S) * N_MOD + k, 0, 0)),
                  pl.BlockSpec((1, d), lambda i: (0, 0))],
        out_specs=pl.BlockSpec((tm, d), lambda i: (i, 0)),
        out_shape=jax.ShapeDtypeStruct((dm.NL, d), F32),
        scratch_shapes=[pltpu.VMEM((2, 2, tm, d), F32), pltpu.SemaphoreType.DMA((2,))],
        compiler_params=_params(("arbitrary",), est),
        name="moe_combine_residual",
    )(p1, p2, p1, p2, y_sorted, g1.reshape(dm.NL, 1), g2.reshape(dm.NL, 1), x, mod3, g_post.reshape(1, d))


def _moe_plan(dm, rout, tm):
    t = dm.NL
    e_flat = jnp.concatenate([rout[:, 0], rout[:, 1]]).astype(jnp.int32)
    onehot = (e_flat[:, None] == jnp.arange(dm.E, dtype=jnp.int32)[None, :]).astype(jnp.int32)
    before = jnp.cumsum(onehot, axis=0) - onehot
    rank = jnp.sum(before * onehot, axis=1)
    counts = jnp.sum(onehot, axis=0)
    tiles = (counts + tm - 1) // tm
    tile_end = jnp.cumsum(tiles)
    tile_start = tile_end - tiles
    dest = jnp.sum(onehot * tile_start[None, :], axis=1) * tm + rank
    nt_max = (2 * t) // tm + dm.E
    n_tiles = tile_end[-1:].astype(jnp.int32)
    tile_ids = jnp.arange(nt_max, dtype=jnp.int32)
    tile_expert = jnp.minimum(jnp.sum((tile_ids[:, None] >= tile_end[None, :]).astype(jnp.int32), axis=1),
                              dm.E - 1).astype(jnp.int32)
    ids = jnp.arange(dm.E, dtype=jnp.int32)
    later = jnp.logical_and(tiles[None, :] > 0, ids[None, :] > ids[:, None])
    next_of = jnp.min(jnp.where(later, ids[None, :], dm.E), axis=1)
    next_of = jnp.where(next_of == dm.E, -1, next_of).astype(jnp.int32)
    next_expert = jnp.sum(jnp.where(tile_expert[:, None] == ids[None, :], next_of[None, :], 0), axis=1)
    tok = jnp.concatenate([jnp.arange(t, dtype=jnp.int32)] * 2)
    src_tok = jnp.zeros((nt_max * tm,), jnp.int32).at[dest].set(tok)
    return src_tok, dest[:t], dest[t:], (tile_expert, next_expert.astype(jnp.int32), n_tiles), nt_max


def _rope_tables(dm):
    half = HEAD_DIM // 2
    freqs = ROPE_THETA ** (-jnp.arange(0, half, 2, dtype=F32) / half)
    pos = jnp.arange(dm.S, dtype=jnp.int32)
    row = (pos // dm.GW).astype(F32)[:, None] * freqs
    col = (pos % dm.GW).astype(F32)[:, None] * freqs
    cos = jnp.concatenate([jnp.cos(row), jnp.cos(row), jnp.cos(col), jnp.cos(col)], axis=1)
    sin = jnp.concatenate([-jnp.sin(row), jnp.sin(row), -jnp.sin(col), jnp.sin(col)], axis=1)
    cos = jnp.concatenate([jnp.tile(cos, (dm.B, 1)), jnp.ones((dm.NC, HEAD_DIM), F32)], axis=0)
    sin = jnp.concatenate([jnp.tile(sin, (dm.B, 1)), jnp.zeros((dm.NC, HEAD_DIM), F32)], axis=0)
    return cos, sin


def _trunk(dm, x, c, ctx, c_ctx, ada_w, ada_b, norm_mix_pre, norm_mix_post, norm_ffn_pre, norm_ffn_post,
           w_in, attn_sink, attn_norm, conv_w, conv_b, dt_bias, a_log, d_skip, ssm_norm, w_out,
           ffn_w_gate, ffn_w_up, ffn_w_down, moe_router, moe_w_gate, moe_w_up, moe_w_down):
    depth = ada_w.shape[0]
    d = dm.D
    x2 = x.reshape(dm.NL, d)
    c2 = ctx.reshape(dm.NC, d)
    cvec = jnp.concatenate([c, c_ctx[None, :], jnp.zeros((MOD_ROWS - dm.B - 1, d), F32)], axis=0)
    mod3 = _mod_call(cvec, ada_w, ada_b).reshape(depth * MOD_ROWS * N_MOD, 1, d)
    cos, sin = _rope_tables(dm)

    tm_all = _row_tile(dm.M, MM_ROW_TILE_CAP)
    tm_lat = _row_tile(dm.NL, MM_ROW_TILE_CAP)
    tn_in = _pick(math.gcd(math.gcd(dm.AD, dm.SD), math.gcd(dm.KD, dm.CD)), (512, 256, 128))
    nq, nz, nk, nx = dm.AD // tn_in, dm.SD // tn_in, dm.KD // tn_in, dm.CD // tn_in
    src_qk = lambda j: jnp.where(j < nq, j, j + nz)
    src_zxv = lambda j: jnp.where(j < nz, j + nq,
                                  jnp.where(j < nz + nx, j + nq + 2 * nk, j - nx + nq + nk))
    w_in_t = jnp.swapaxes(w_in, 1, 2)
    w_dt = w_in_t[:, dm.DT0:, :].reshape(depth, N_DIR, dm.SH, d)
    w_dt = jnp.pad(w_dt, ((0, 0), (0, 0), (0, LANES - dm.SH), (0, 0))).reshape(depth, N_DIR * LANES, d)

    xa = None
    for i in range(depth):
        last = i == depth - 1
        if i == 0:
            _, h = _norm_call(dm, dm.M, x2, c2, None, None, None, norm_mix_pre[i], (i, 0), (i, 1), mod3, BF16)
        qk = _mm_call([h], [w_in_t], i, rows=dm.M, tm=tm_all, tn=tn_in, n_cols=dm.AD + dm.KD, out_dtype=BF16,
                      rope=(cos, sin), w_nt=True, panel_src=src_qk, name="proj_in_qk")
        big = _mm_call([h], [w_in_t], i, rows=dm.M, tm=tm_all, tn=tn_in, n_cols=dm.SD + dm.CD + dm.KD,
                       out_dtype=BF16, w_nt=True, panel_src=src_zxv, name="proj_in_zxv")
        dt_pad = _mm_call([h], [w_dt], i, rows=dm.M, tm=tm_all, tn=N_DIR * LANES, n_cols=N_DIR * LANES,
                          out_dtype=F32, w_nt=True, name="proj_dt")
        u = _conv_call(dm, big, conv_w[i], conv_b[i])
        y2 = _ssd_call(dm, u, dt_pad, dt_bias[i], a_log[i], d_skip[i])
        rows = dm.NL if last else dm.M
        tm_rows = tm_lat if last else tm_all
        ssm = _gate_call(dm, rows, y2, big, ssm_norm[i])
        attn = _attn_call(dm, qk, big, attn_sink[i], attn_norm[i], with_ctx=not last)
        y = _mm_call([attn, ssm], [w_out], i, rows=rows, tm=tm_rows, tn=_pick(d, (512, 256, 128)), n_cols=d,
                     out_dtype=BF16, name="proj_out")
        if i == 0:
            xa, hf = _norm_call(dm, rows, x2, c2, y, norm_mix_post[i], (i, 2), norm_ffn_pre[i], (i, 3), (i, 4),
                                mod3, BF16 if i % 2 == 0 else F32)
        else:
            xa, hf = _norm_call(dm, rows, xa, None, y, norm_mix_post[i], (i, 2), norm_ffn_pre[i], (i, 3), (i, 4),
                                mod3, BF16 if i % 2 == 0 else F32)
        j = i // 2
        if i % 2 == 0:
            act = _mm_call([hf], [ffn_w_gate, ffn_w_up], j, rows=rows, tm=tm_rows,
                           tn=_pick(dm.DFF, (256, 128)), n_cols=dm.DFF, out_dtype=BF16, mode="swiglu",
                           name="ffn_gate_up")
            f = _mm_call([act], [ffn_w_down], j, rows=rows, tm=_row_tile(rows, MM_ROW_TILE_CAP // 4),
                         tn=_pick(d, (512, 256, 128)), n_cols=d, out_dtype=BF16, name="ffn_down")
            if last:
                raise NotImplementedError("dense FFN on the last layer")
            xa, h = _norm_call(dm, rows, xa, None, f, norm_ffn_post[i], (i, 5), norm_mix_pre[i + 1],
                               (i + 1, 0), (i + 1, 1), mod3, BF16)
        else:
            if not last:
                raise NotImplementedError("MoE FFN on a non-final layer")
            tm_e = _pick(dm.NL, (256, 128))
            rout = _router_call(dm, hf, moe_router[j])
            src_tok, pos1, pos2, tables, nt_max = _moe_plan(dm, rout, tm_e)
            xs = _gather_call(dm, hf, src_tok, tables[2], tm_e, nt_max)
            n_exp = moe_w_gate.shape[1]
            wg = moe_w_gate.reshape((-1,) + moe_w_gate.shape[2:])
            wu = moe_w_up.reshape((-1,) + moe_w_up.shape[2:])
            wd = moe_w_down.reshape((-1,) + moe_w_down.shape[2:])
            act = _mm_call([xs], [wg, wu], j * n_exp, rows=nt_max * tm_e, tm=tm_e,
                           tn=_pick(dm.DE, (512, 256, 128)), n_cols=dm.DE, out_dtype=BF16, mode="swiglu",
                           grouped=tables, name="moe_gate_up")
            ys = _mm_call([act], [wd], j * n_exp, rows=nt_max * tm_e, tm=tm_e,
                          tn=_pick(d, (512, 256, 128)), n_cols=d, out_dtype=F32,
                          grouped=tables, name="moe_down")
            xa = _combine_call(dm, ys, pos1, pos2, rout[:, 2], rout[:, 3], xa, mod3, (i, 5), norm_ffn_post[i],
                               tm_e)
    return xa[:dm.NL].reshape(dm.B, dm.S, d)


def kernel(x, c, ctx, c_ctx, ada_w, ada_b, norm_mix_pre, norm_mix_post, norm_ffn_pre, norm_ffn_post, w_in,
           attn_sink, attn_norm, conv_w, conv_b, dt_bias, a_log, d_skip, ssm_norm, w_out, ffn_w_gate, ffn_w_up,
           ffn_w_down, moe_router, moe_w_gate, moe_w_up, moe_w_down):
    b, s, d = x.shape
    sh = a_log.shape[2]
    cd = conv_w.shape[2]
    sd = sh * SSM_HEAD_DIM
    dm = Dims(B=b, S=s, L=ctx.shape[1], D=d, AH=attn_sink.shape[1], KVH=ATTN_KV_HEADS, SH=sh,
              SG=(cd - sd) // (2 * SSM_STATE), DFF=ffn_w_gate.shape[2], E=moe_router.shape[2],
              DE=moe_w_gate.shape[3], GW=GRID_W)
    assert w_in.shape[2] == dm.DT0 + N_DIR * dm.SH and w_out.shape[1] == dm.AD + dm.SD and dm.AD == dm.SD
    assert dm.S % BLOCK == 0 and dm.L % BLOCK == 0 and dm.SH <= LANES and dm.E <= LANES
    return _trunk(dm, x, c, ctx, c_ctx, ada_w, ada_b, norm_mix_pre, norm_mix_post, norm_ffn_pre, norm_ffn_post,
                  w_in, attn_sink, attn_norm, conv_w, conv_b, dt_bias, a_log, d_skip, ssm_norm, w_out,
                  ffn_w_gate, ffn_w_up, ffn_w_down, moe_router, moe_w_gate, moe_w_up, moe_w_down)
```

```python
import functools
import math
from typing import NamedTuple

import jax
import jax.numpy as jnp
from jax import lax
from jax.experimental import pallas as pl
from jax.experimental.pallas import tpu as pltpu

F32 = jnp.float32
BF16 = jnp.bfloat16

HEAD_DIM = 128
WINDOW = 128
BLOCK = 128
ROPE_THETA = 10000.0
SSM_HEAD_DIM = 64
SSM_STATE = 128
CONV_K = 5
CHUNK = 128
N_DIR = 2
GRID_W = 64
ATTN_KV_HEADS = 4
N_MOD = 6
EPS = 1e-6
MOD_ROWS = 8

V7X_VMEM_BYTES = 64 * 1024 * 1024
VMEM_LIMIT_CAP = 56 * 1024 * 1024
LANES = 128
BF16_SUBLANES = 16
MM_ROW_TILE_CAP = 1100


class Dims(NamedTuple):
    B: int
    S: int
    L: int
    D: int
    AH: int
    KVH: int
    SH: int
    SG: int
    DFF: int
    E: int
    DE: int
    GW: int

    @property
    def AD(self):
        return self.AH * HEAD_DIM

    @property
    def KD(self):
        return self.KVH * HEAD_DIM

    @property
    def SD(self):
        return self.SH * SSM_HEAD_DIM

    @property
    def BC(self):
        return self.SG * SSM_STATE

    @property
    def CD(self):
        return self.SD + 2 * self.BC

    @property
    def K0(self):
        return self.AD + self.SD

    @property
    def V0(self):
        return self.K0 + self.KD

    @property
    def X0(self):
        return self.V0 + self.KD

    @property
    def DT0(self):
        return self.X0 + self.CD

    @property
    def NL(self):
        return self.B * self.S

    @property
    def NC(self):
        return self.B * self.L

    @property
    def M(self):
        return self.NL + self.NC


def _pick(n, prefs):
    for p in prefs:
        if n % p == 0:
            return p
    raise ValueError(f"no tile in {prefs} divides {n}")


def _row_tile(rows, cap):
    for t in range(min(cap, rows), 0, -1):
        if rows % t == 0 and t % BF16_SUBLANES == 0:
            return t
    raise ValueError(f"no row tile <= {cap} divides {rows}")


def _params(sem, est_bytes):
    limit = int(min(max(est_bytes * 5 // 4 + (2 << 20), 16 << 20), VMEM_LIMIT_CAP))
    return pltpu.CompilerParams(dimension_semantics=sem, vmem_limit_bytes=limit)


def _silu(v):
    return v / (1.0 + jnp.exp(-v))


def _rms(v, g):
    return v * lax.rsqrt(jnp.mean(v * v, axis=-1, keepdims=True) + EPS) * g


def _mod_kernel(c_ref, w_ref, b_ref, o_ref):
    c = _silu(c_ref[...]).astype(BF16)
    o_ref[...] = jnp.dot(c, w_ref[...].astype(BF16), preferred_element_type=F32) + b_ref[...]


def _mod_call(cvec, ada_w, ada_b):
    depth, d, n = ada_w.shape
    tn = _pick(n, (512, 256, 128))
    est = 2 * d * tn * 4 + d * tn * 2 + 4 * MOD_ROWS * tn * 4
    return pl.pallas_call(
        _mod_kernel,
        grid=(depth, n // tn),
        in_specs=[
            pl.BlockSpec((MOD_ROWS, d), lambda l, j: (0, 0)),
            pl.BlockSpec((None, d, tn), lambda l, j: (l, 0, j)),
            pl.BlockSpec((None, 1, tn), lambda l, j: (l, 0, j)),
        ],
        out_specs=pl.BlockSpec((None, MOD_ROWS, tn), lambda l, j: (l, 0, j)),
        out_shape=jax.ShapeDtypeStruct((depth, MOD_ROWS, n), F32),
        compiler_params=_params(("arbitrary", "arbitrary"), est),
        name="adaln_mod",
    )(cvec, ada_w, ada_b.reshape(depth, 1, n))


def _norm_kernel(*refs, split, has_res, n_lat_blocks):
    refs = list(refs)
    x_ref = refs.pop(0)
    c_ref = refs.pop(0) if split else None
    if has_res:
        y_ref, gpost_ref, gate_ref = refs.pop(0), refs.pop(0), refs.pop(0)
    gpre_ref, sh_ref, sc_ref = refs.pop(0), refs.pop(0), refs.pop(0)
    if has_res:
        xo_ref = refs.pop(0)
    h_ref = refs.pop(0)

    def run(src_ref):
        x = src_ref[...]
        if has_res:
            x = x + gate_ref[...] * _rms(y_ref[...].astype(F32), gpost_ref[...])
            xo_ref[...] = x
        h_ref[...] = (_rms(x, gpre_ref[...]) * (1.0 + sc_ref[...]) + sh_ref[...]).astype(h_ref.dtype)

    if split:
        i = pl.program_id(0)
        pl.when(i < n_lat_blocks)(lambda: run(x_ref))
        pl.when(i >= n_lat_blocks)(lambda: run(c_ref))
    else:
        run(x_ref)


def _norm_call(dm, rows, x, ctx2d, y, g_post, gate_lk, g_pre, sh_lk, sc_lk, mod3, h_dtype):
    d = dm.D
    tm = _pick(dm.L, (256, 128))
    nblk = rows // tm
    nlat = dm.NL // tm
    split = ctx2d is not None
    has_res = y is not None

    def mod_spec(lk):
        layer, k = lk
        return pl.BlockSpec(
            (None, 1, d),
            lambda i: ((layer * MOD_ROWS + jnp.minimum((i * tm) // dm.S, dm.B)) * N_MOD + k, 0, 0))

    row_spec = pl.BlockSpec((tm, d), lambda i: (i, 0))
    vec_spec = pl.BlockSpec((1, d), lambda i: (0, 0))
    in_specs, args = [], []
    if split:
        in_specs += [pl.BlockSpec((tm, d), lambda i: (jnp.minimum(i, nlat - 1), 0)),
                     pl.BlockSpec((tm, d), lambda i: (jnp.maximum(i - nlat, 0), 0))]
        args += [x, ctx2d]
    else:
        in_specs += [row_spec]
        args += [x]
    if has_res:
        in_specs += [row_spec, vec_spec, mod_spec(gate_lk)]
        args += [y, g_post.reshape(1, d), mod3]
    in_specs += [vec_spec, mod_spec(sh_lk), mod_spec(sc_lk)]
    args += [g_pre.reshape(1, d), mod3, mod3]
    out_specs, out_shape = [], []
    if has_res:
        out_specs.append(row_spec)
        out_shape.append(jax.ShapeDtypeStruct((rows, d), F32))
    out_specs.append(row_spec)
    out_shape.append(jax.ShapeDtypeStruct((rows, d), h_dtype))
    est = 2 * tm * d * (4 + 4 + 2 + 4 + 4)
    outs = pl.pallas_call(
        functools.partial(_norm_kernel, split=split, has_res=has_res, n_lat_blocks=nlat),
        grid=(nblk,),
        in_specs=in_specs,
        out_specs=out_specs,
        out_shape=out_shape,
        compiler_params=_params(("arbitrary",), est),
        name="residual_prenorm" if has_res else "prenorm",
    )(*args)
    return outs if has_res else (None, outs[0])


def _rope(acc, cos, sin_signed, tn):
    lane = lax.broadcasted_iota(jnp.int32, acc.shape, 1)
    first = (lane % 64) < 32
    partner = jnp.where(first, pltpu.roll(acc, tn - 32, 1), pltpu.roll(acc, 32, 1))
    reps = tn // HEAD_DIM
    cos_t = jnp.concatenate([cos] * reps, axis=1) if reps > 1 else cos
    sin_t = jnp.concatenate([sin_signed] * reps, axis=1) if reps > 1 else sin_signed
    return acc * cos_t + partner * sin_t


def _mm_kernel(*refs, n_lhs, k_sizes, n_w, mode, with_rope, grouped, tn, w_nt, w_group, n_panels, panel_src):
    refs = list(refs)
    if grouped:
        te_ref, nxt_ref, nt_ref = refs.pop(0), refs.pop(0), refs.pop(0)
    lhs_refs = [refs.pop(0) for _ in range(n_lhs)]
    w_refs = [refs.pop(0) for _ in range(n_w)]
    if with_rope:
        cos_ref, sin_ref = refs.pop(0), refs.pop(0)
    o_ref = refs.pop(0)
    wb_refs = [refs.pop(0) for _ in range(n_w)]
    stage_refs = [refs.pop(0) for _ in range(n_w)]
    sem = refs.pop(0)
    j = pl.program_id(0)
    m = pl.program_id(1)

    def panel_copy(i, g, jp):
        col = pl.multiple_of(panel_src(jp) * tn, tn)
        src = w_refs[i].at[g, pl.ds(col, tn), :] if w_nt else w_refs[i].at[g, :, pl.ds(col, tn)]
        return pltpu.make_async_copy(src, stage_refs[i], sem.at[i])

    if grouped:
        nt = nt_ref[0]
        active = m < nt
        new_panel = jnp.logical_or(m == 0, jnp.logical_and(active, te_ref[m] != te_ref[jnp.maximum(m - 1, 0)]))
        g_cur = w_group + te_ref[m]
        more_here = nxt_ref[m] >= 0
        g_next = w_group + jnp.where(more_here, nxt_ref[m], te_ref[0])
        j_next = jnp.where(more_here, j, j + 1)
        has_next = jnp.logical_or(more_here, j + 1 < n_panels)
    else:
        active = None
        new_panel = m == 0
        g_cur = g_next = w_group
        j_next = j + 1
        has_next = j + 1 < n_panels

    @pl.when(jnp.logical_and(j == 0, m == 0))
    def _():
        for i in range(n_w):
            panel_copy(i, g_cur, j).start()

    @pl.when(new_panel)
    def _():
        for i in range(n_w):
            panel_copy(i, g_cur, j).wait()
            wb_refs[i][...] = stage_refs[i][...].astype(BF16)

        @pl.when(has_next)
        def _():
            for i in range(n_w):
                panel_copy(i, g_next, j_next).start(priority=1)

    def compute():
        accs = []
        for wb_ref in wb_refs:
            acc, k0 = None, 0
            for l_ref, ks in zip(lhs_refs, k_sizes):
                if w_nt:
                    part = lax.dot_general(l_ref[...], wb_ref[:, k0:k0 + ks], (((1,), (1,)), ((), ())),
                                           preferred_element_type=F32)
                else:
                    part = jnp.dot(l_ref[...], wb_ref[k0:k0 + ks, :], preferred_element_type=F32)
                acc = part if acc is None else acc + part
                k0 += ks
            accs.append(acc)
        res = _silu(accs[0]) * accs[1] if mode == "swiglu" else accs[0]
        if with_rope:
            res = _rope(res, cos_ref[...], sin_ref[...], tn)
        o_ref[...] = res.astype(o_ref.dtype)

    if grouped:
        pl.when(active)(compute)

        @pl.when(jnp.logical_not(active))
        def _():
            o_ref[...] = jnp.zeros_like(o_ref)
    else:
        compute()


def _mm_call(lhs_list, w_list, w_group, *, rows, tm, tn, n_cols, out_dtype, mode="plain",
             rope=None, grouped=None, w_nt=False, panel_src=None, name="panel_matmul"):
    k_sizes = tuple(a.shape[1] for a in lhs_list)
    k_tot = sum(k_sizes)
    n_w = len(w_list)
    assert all(w.shape[2 if w_nt else 1] == k_tot for w in w_list)
    n_panels = n_cols // tn
    n_blocks = rows // tm
    args, in_specs = [], []
    src = panel_src if panel_src is not None else (lambda j: j)
    if grouped is not None:
        lhs_map = lambda j, m, te, nx, nt: (jnp.minimum(m, nt[0] - 1), 0)
        out_map = lambda j, m, te, nx, nt: (m, j)
    else:
        lhs_map = lambda j, m: (m, 0)
        out_map = lambda j, m: (m, j)
    for a, ks in zip(lhs_list, k_sizes):
        in_specs.append(pl.BlockSpec((tm, ks), lhs_map))
        args.append(a)
    w_block = (tn, k_tot) if w_nt else (k_tot, tn)
    for w in w_list:
        in_specs.append(pl.BlockSpec(memory_space=pl.ANY))
        args.append(w)
    if rope is not None:
        cos, sin_signed = rope
        in_specs += [pl.BlockSpec((tm, HEAD_DIM), lambda j, m: (m, 0))] * 2
        args += [cos, sin_signed]
    out_bytes = jnp.dtype(out_dtype).itemsize
    est = (n_w * k_tot * tn * (4 + 2) + 2 * tm * k_tot * 2 + 2 * tm * tn * out_bytes + (2 + n_w) * tm * tn * 4)
    kern = functools.partial(_mm_kernel, n_lhs=len(lhs_list), k_sizes=k_sizes, n_w=n_w, mode=mode,
                             with_rope=rope is not None, grouped=grouped is not None, tn=tn, w_nt=w_nt,
                             w_group=w_group, n_panels=n_panels, panel_src=src)
    scratch = ([pltpu.VMEM(w_block, BF16) for _ in range(n_w)] + [pltpu.VMEM(w_block, F32) for _ in range(n_w)]
               + [pltpu.SemaphoreType.DMA((n_w,))])
    out_shape = jax.ShapeDtypeStruct((rows, n_cols), out_dtype)
    cp = _params(("arbitrary", "arbitrary"), est)
    if grouped is not None:
        grid_spec = pltpu.PrefetchScalarGridSpec(
            num_scalar_prefetch=3, grid=(n_panels, n_blocks), in_specs=in_specs,
            out_specs=pl.BlockSpec((tm, tn), out_map), scratch_shapes=scratch)
        return pl.pallas_call(kern, grid_spec=grid_spec, out_shape=out_shape, compiler_params=cp,
                              name=name)(*grouped, *args)
    return pl.pallas_call(kern, grid=(n_panels, n_blocks), in_specs=in_specs,
                          out_specs=pl.BlockSpec((tm, tn), out_map), out_shape=out_shape,
                          scratch_shapes=scratch, compiler_params=cp, name=name)(*args)


def _conv_kernel(prev_ref, cur_ref, next_ref, w_ref, b_ref, o_ref, *, tb, seg_lat, seg_ctx, n_lat_rows):
    i = pl.program_id(0)
    row0 = i * tb
    in_lat = row0 < n_lat_rows
    off = jnp.where(in_lat, row0 % seg_lat, (row0 - n_lat_rows) % seg_ctx)
    seg = jnp.where(in_lat, seg_lat, seg_ctx)
    keep_prev = jnp.where(off != 0, 1.0, 0.0)
    keep_next = jnp.where(off + tb != seg, 1.0, 0.0)
    halo = BF16_SUBLANES
    prev = prev_ref[...].astype(F32)[halo - 8:, :] * keep_prev
    nxt = next_ref[...].astype(F32)[:8, :] * keep_next
    ext = jnp.concatenate([prev, cur_ref[...].astype(F32), nxt], axis=0)
    rows = tb + 16
    pad = CONV_K // 2
    acc = None
    for k in range(CONV_K):
        shift = (pad - k) % rows
        shifted = ext if shift == 0 else pltpu.roll(ext, shift, 0)
        term = shifted[8:8 + tb, :] * w_ref[k:k + 1, :]
        acc = term if acc is None else acc + term
    o_ref[...] = _silu(acc + b_ref[...]).astype(o_ref.dtype)


def _conv_call(dm, big, conv_w, conv_b):
    tb = _pick(dm.L, (256, 128))
    cw = _pick(math.gcd(dm.SD, dm.CD), (1024, 512, 256, 128))
    halo = BF16_SUBLANES
    col0 = dm.SD // cw
    nrb = dm.M // tb
    last_halo = dm.M // halo - 1
    est = 2 * (tb + 2 * halo) * cw * 2 + 2 * tb * cw * 2 + 8 * (tb + 16) * cw * 4
    return pl.pallas_call(
        functools.partial(_conv_kernel, tb=tb, seg_lat=dm.S, seg_ctx=dm.L, n_lat_rows=dm.NL),
        grid=(nrb, dm.CD // cw),
        in_specs=[
            pl.BlockSpec((halo, cw), lambda i, j: (jnp.maximum(i * (tb // halo) - 1, 0), col0 + j)),
            pl.BlockSpec((tb, cw), lambda i, j: (i, col0 + j)),
            pl.BlockSpec((halo, cw), lambda i, j: (jnp.minimum((i + 1) * (tb // halo), last_halo), col0 + j)),
            pl.BlockSpec((8, cw), lambda i, j: (0, j)),
            pl.BlockSpec((1, cw), lambda i, j: (0, j)),
        ],
        out_specs=pl.BlockSpec((tb, cw), lambda i, j: (i, j)),
        out_shape=jax.ShapeDtypeStruct((dm.M, dm.CD), BF16),
        compiler_params=_params(("arbitrary", "arbitrary"), est),
        name="dwconv_silu",
    )(big, big, big, jnp.pad(conv_w, ((0, 8 - CONV_K), (0, 0))), conv_b.reshape(1, dm.CD))


def _split3(v):
    hi = v.astype(BF16)
    r1 = v - hi.astype(F32)
    mid = r1.astype(BF16)
    lo = (r1 - mid.astype(F32)).astype(BF16)
    return hi, mid, lo


def _softplus(v):
    t = jnp.exp(-jnp.abs(v))
    y = 1.0 + t
    return jnp.maximum(v, 0.0) + (jnp.log(y) - ((y - 1.0) - t) / y)


def _ssd_kernel(u_ref, dt_ref, bias_ref, alog_ref, dsk_ref, y_ref, st_ref, *, SH, SG):
    d = pl.program_id(1)
    s = pl.program_id(2)
    hpg = SH // SG
    sd = SH * SSM_HEAD_DIM
    bc = SG * SSM_STATE
    c = CHUNK

    @pl.when(s == 0)
    def _():
        st_ref[...] = jnp.zeros_like(st_ref)

    dt = _softplus(dt_ref[...] + bias_ref[...])
    a = dt * (-jnp.exp(alog_ref[...]))
    row = lax.broadcasted_iota(jnp.int32, (c, c), 0)
    col = lax.broadcasted_iota(jnp.int32, (c, c), 1)
    fwd = d == 0
    tri = jnp.where(fwd, col, row) <= jnp.where(fwd, row, col)
    hi, mid, lo = _split3(a)
    tri_b = jnp.where(tri, 1.0, 0.0).astype(BF16)
    parts = jnp.dot(tri_b, jnp.concatenate([hi, mid, lo], axis=1), preferred_element_type=F32)
    cum = parts[:, :LANES] + parts[:, LANES:2 * LANES] + parts[:, 2 * LANES:]
    tot = jnp.sum(a, axis=0, keepdims=True)
    w_t = (jnp.exp(tot - cum) * dt).T
    cum_t = cum.T
    dt_t = dt.T
    etot = jnp.exp(tot)
    lane_lo = lax.broadcasted_iota(jnp.int32, (c, LANES), 1) < SSM_HEAD_DIM
    lane_lo1 = lax.broadcasted_iota(jnp.int32, (1, LANES), 1) < SSM_HEAD_DIM
    neg_inf = jnp.float32(-jnp.inf)

    for g in range(SG):
        b_g = u_ref[:, sd + g * SSM_STATE: sd + (g + 1) * SSM_STATE]
        c_g = u_ref[:, sd + bc + g * SSM_STATE: sd + bc + (g + 1) * SSM_STATE]
        cb = lax.dot_general(c_g, b_g, (((1,), (1,)), ((), ())), preferred_element_type=F32)
        bt_g = b_g.astype(F32).T
        c_g32 = c_g.astype(F32)
        for pr in range(hpg // 2):
            h0 = g * hpg + 2 * pr
            lo_col, hi_col = h0 * SSM_HEAD_DIM, (h0 + 2) * SSM_HEAD_DIM
            xs_pair = u_ref[:, lo_col:hi_col]
            st_pair = st_ref[:, lo_col:hi_col]
            rhs = jnp.concatenate([xs_pair, st_pair.astype(BF16)], axis=0)
            ys, sts, ets = [], [], []
            for q in range(2):
                h = h0 + q
                colb = jnp.broadcast_to(cum[:, h:h + 1], (c, c))
                decay = jnp.exp(jnp.where(tri, colb - cum_t[h:h + 1, :], neg_inf))
                m_h = decay * cb * dt_t[h:h + 1, :]
                c_h = c_g32 * jnp.exp(colb)
                lhs = jnp.concatenate([m_h, c_h], axis=1).astype(BF16)
                ys.append(jnp.dot(lhs, rhs, preferred_element_type=F32))
                btw = (bt_g * w_t[h:h + 1, :]).astype(BF16)
                sts.append(jnp.dot(btw, xs_pair, preferred_element_type=F32))
                ets.append(jnp.broadcast_to(etot[:, h:h + 1], (1, LANES)))
            y_pair = jnp.where(lane_lo, ys[0], ys[1]) + dsk_ref[:, lo_col:hi_col] * xs_pair.astype(F32)
            e_pair = jnp.where(lane_lo1, ets[0], ets[1])
            st_ref[:, lo_col:hi_col] = e_pair * st_pair + jnp.where(lane_lo, sts[0], sts[1])
            y_ref[:, lo_col:hi_col] = y_pair.astype(y_ref.dtype)


def _ssd_call(dm, u, dt_pad, dt_bias, a_log, d_skip):
    nl, ns = dm.L // CHUNK, dm.S // CHUNK
    lat_blocks = dm.NL // CHUNK

    def rb(b, d, s):
        ctx_chunk = jnp.where(d == 0, s, nl - 1 - s)
        lat_chunk = jnp.where(d == 0, s - nl, ns - 1 - (s - nl))
        return jnp.where(s < nl, lat_blocks + b * nl + ctx_chunk, b * ns + lat_chunk)

    pad = LANES - dm.SH
    bias_p = jnp.pad(dt_bias, ((0, 0), (0, pad))).reshape(N_DIR, 1, LANES)
    alog_p = jnp.pad(a_log, ((0, 0), (0, pad))).reshape(N_DIR, 1, LANES)
    dsk = jnp.repeat(d_skip, SSM_HEAD_DIM)
    dsk_p = jnp.stack([dsk, jnp.zeros_like(dsk)]).reshape(N_DIR, 1, dm.SD)
    est = 2 * CHUNK * dm.CD * 2 + 2 * CHUNK * dm.SD * 2 + SSM_STATE * dm.SD * 4 + (8 << 20)
    return pl.pallas_call(
        functools.partial(_ssd_kernel, SH=dm.SH, SG=dm.SG),
        grid=(dm.B, N_DIR, nl + ns),
        in_specs=[
            pl.BlockSpec((CHUNK, dm.CD), lambda b, d, s: (rb(b, d, s), 0)),
            pl.BlockSpec((CHUNK, LANES), lambda b, d, s: (rb(b, d, s), d)),
            pl.BlockSpec((None, 1, LANES), lambda b, d, s: (d, 0, 0)),
            pl.BlockSpec((None, 1, LANES), lambda b, d, s: (d, 0, 0)),
            pl.BlockSpec((None, 1, dm.SD), lambda b, d, s: (d, 0, 0)),
        ],
        out_specs=pl.BlockSpec((None, CHUNK, dm.SD), lambda b, d, s: (d, rb(b, d, s), 0)),
        out_shape=jax.ShapeDtypeStruct((N_DIR, dm.M, dm.SD), BF16),
        scratch_shapes=[pltpu.VMEM((SSM_STATE, dm.SD), F32)],
        compiler_params=_params(("arbitrary", "arbitrary", "arbitrary"), est),
        name="ssd_scan",
    )(u, dt_pad, bias_p, alog_p, dsk_p)


def _gate_kernel(yf_ref, yb_ref, z_ref, g_ref, o_ref):
    y = yf_ref[...].astype(F32) + yb_ref[...].astype(F32)
    o_ref[...] = _rms(y * _silu(z_ref[...].astype(F32)), g_ref[...]).astype(o_ref.dtype)


def _gate_call(dm, rows, y2, big, ssm_norm):
    tm = _pick(dm.L, (256, 128))
    sd = dm.SD
    zc = 0
    est = 2 * tm * sd * 2 * 4 + 6 * tm * sd * 4
    return pl.pallas_call(
        _gate_kernel,
        grid=(rows // tm,),
        in_specs=[
            pl.BlockSpec((None, tm, sd), lambda i: (0, i, 0)),
            pl.BlockSpec((None, tm, sd), lambda i: (1, i, 0)),
            pl.BlockSpec((tm, sd), lambda i: (i, zc)),
            pl.BlockSpec((1, sd), lambda i: (0, 0)),
        ],
        out_specs=pl.BlockSpec((tm, sd), lambda i: (i, 0)),
        out_shape=jax.ShapeDtypeStruct((rows, sd), BF16),
        compiler_params=_params(("arbitrary",), est),
        name="ssd_gate_norm",
    )(y2, y2, big, ssm_norm.reshape(1, sd))


def _attn_kernel(sink_ref, q_ref, kp_ref, kc_ref, kn_ref, vp_ref, vc_ref, vn_ref, kx_ref, vx_ref, g_ref,
                 o_ref, *, KVH, G, n_lat, L):
    i = pl.program_id(1)
    is_ctx = i >= n_lat
    band = 3 * BLOCK
    width = band + L
    rowq = lax.broadcasted_iota(jnp.int32, (BLOCK, width), 0)
    colk = lax.broadcasted_iota(jnp.int32, (BLOCK, width), 1)
    first_valid = jnp.where(i == 0, BLOCK, 0)
    last_valid = jnp.where(is_ctx, -1, jnp.where(i == n_lat - 1, 2 * BLOCK - 1, band - 1))
    lower = jnp.maximum(rowq, first_valid)
    upper = jnp.minimum(rowq + 2 * WINDOW, last_valid)
    neg_inf = jnp.float32(-jnp.inf)
    in_band = jnp.where(colk >= lower, jnp.where(colk <= upper, 0.0, neg_inf), neg_inf)
    bias = jnp.where(colk >= band, 0.0, in_band)
    scale = HEAD_DIM ** -0.5
    outs = []
    for hk in range(KVH):
        ks = slice(hk * HEAD_DIM, (hk + 1) * HEAD_DIM)
        qh = jnp.concatenate([q_ref[:, (hk * G + g) * HEAD_DIM:(hk * G + g + 1) * HEAD_DIM] for g in range(G)],
                             axis=0)
        kb = jnp.concatenate([kp_ref[:, ks], kc_ref[:, ks], kn_ref[:, ks], kx_ref[:, ks]], axis=0)
        vb = jnp.concatenate([vp_ref[:, ks], vc_ref[:, ks], vn_ref[:, ks], vx_ref[:, ks]], axis=0)
        sc = lax.dot_general(qh, kb, (((1,), (1,)), ((), ())), preferred_element_type=F32) * scale
        ps, dens = [], []
        for g in range(G):
            sg = sc[g * BLOCK:(g + 1) * BLOCK, :] + bias
            sk = sink_ref[hk * G + g]
            mx = jnp.maximum(jnp.max(sg, axis=-1, keepdims=True), sk)
            p = jnp.exp(sg - mx)
            dens.append(jnp.sum(p, axis=-1, keepdims=True) + jnp.exp(sk - mx))
            ps.append(p.astype(BF16))
        pv = jnp.dot(jnp.concatenate(ps, axis=0), vb, preferred_element_type=F32)
        for g in range(G):
            outs.append(pv[g * BLOCK:(g + 1) * BLOCK, :] / dens[g])
    o = jnp.concatenate(outs, axis=1)
    o_ref[...] = _rms(o, g_ref[...]).astype(o_ref.dtype)


def _attn_call(dm, qk, big, sink, attn_norm, with_ctx):
    ns, nl = dm.S // BLOCK, dm.L // BLOCK
    lat_blocks = dm.NL // BLOCK
    steps = ns + (nl if with_ctx else 0)
    rows = dm.M if with_ctx else dm.NL
    kd, ad = dm.KD, dm.AD
    kcol, vcol = ad // kd, (dm.SD + dm.CD) // kd

    def qrow(b, i):
        return jnp.where(i < ns, b * ns + i, lat_blocks + b * nl + (i - ns))

    def band_row(b, i, off):
        return b * ns + jnp.clip(i + off, 0, ns - 1)

    def band_spec(col, off):
        return pl.BlockSpec((BLOCK, kd), lambda b, i: (band_row(b, i, off), col))

    ctx_row0 = dm.NL // dm.L
    est = 2 * BLOCK * ad * 2 * 2 + 2 * 6 * BLOCK * kd * 2 + 2 * 2 * dm.L * kd * 2 + (12 << 20)
    return pl.pallas_call(
        functools.partial(_attn_kernel, KVH=dm.KVH, G=dm.AH // dm.KVH, n_lat=ns, L=dm.L),
        grid=(dm.B, steps),
        in_specs=[
            pl.BlockSpec(memory_space=pltpu.SMEM),
            pl.BlockSpec((BLOCK, ad), lambda b, i: (qrow(b, i), 0)),
            band_spec(kcol, -1), band_spec(kcol, 0), band_spec(kcol, 1),
            band_spec(vcol, -1), band_spec(vcol, 0), band_spec(vcol, 1),
            pl.BlockSpec((dm.L, kd), lambda b, i: (ctx_row0 + b, kcol)),
            pl.BlockSpec((dm.L, kd), lambda b, i: (ctx_row0 + b, vcol)),
            pl.BlockSpec((1, ad), lambda b, i: (0, 0)),
        ],
        out_specs=pl.BlockSpec((BLOCK, ad), lambda b, i: (qrow(b, i), 0)),
        out_shape=jax.ShapeDtypeStruct((rows, ad), BF16),
        compiler_params=_params(("arbitrary", "arbitrary"), est),
        name="band_attention",
    )(sink, qk, qk, qk, qk, big, big, big, qk, big, attn_norm.reshape(1, ad))


def _router_kernel(h_ref, r_ref, o_ref, *, E):
    x = h_ref[...]
    hi = x.astype(BF16)
    lo = (x - hi.astype(F32)).astype(BF16)
    r = r_ref[...]
    a = jnp.dot(hi, r, preferred_element_type=F32)
    b = jnp.dot(lo, r, preferred_element_type=F32)
    logits = (a[:, :LANES] + a[:, LANES:]) + (b[:, :LANES] + b[:, LANES:])
    lane = lax.broadcasted_iota(jnp.int32, logits.shape, 1)
    neg_inf = jnp.float32(-jnp.inf)
    lg = jnp.where(lane < E, logits, neg_inf)
    m1 = jnp.max(lg, axis=-1, keepdims=True)
    i1 = jnp.min(jnp.where(lg == m1, lane, LANES), axis=-1, keepdims=True)
    lg2 = jnp.where(lane == i1, neg_inf, lg)
    m2 = jnp.max(lg2, axis=-1, keepdims=True)
    i2 = jnp.min(jnp.where(lg2 == m2, lane, LANES), axis=-1, keepdims=True)
    e = jnp.exp(m2 - m1)
    g1 = 1.0 / (1.0 + e)
    g2 = e / (1.0 + e)
    o_ref[...] = jnp.where(lane == 0, i1.astype(F32),
                           jnp.where(lane == 1, i2.astype(F32),
                                     jnp.where(lane == 2, g1, jnp.where(lane == 3, g2, 0.0))))


def _router_call(dm, hf, router):
    tm = _pick(dm.NL, (512, 256, 128))
    r = jnp.pad(router, ((0, 0), (0, LANES - dm.E)))
    r_hi = r.astype(BF16)
    r_lo = (r - r_hi.astype(F32)).astype(BF16)
    r2 = jnp.concatenate([r_hi, r_lo], axis=1)
    est = 2 * tm * dm.D * 4 + 2 * dm.D * 2 * LANES * 2 + 4 * tm * dm.D * 2
    return pl.pallas_call(
        functools.partial(_router_kernel, E=dm.E),
        grid=(dm.NL // tm,),
        in_specs=[pl.BlockSpec((tm, dm.D), lambda i: (i, 0)),
                  pl.BlockSpec((dm.D, 2 * LANES), lambda i: (0, 0))],
        out_specs=pl.BlockSpec((tm, LANES), lambda i: (i, 0)),
        out_shape=jax.ShapeDtypeStruct((dm.NL, LANES), F32),
        compiler_params=_params(("arbitrary",), est),
        name="moe_router",
    )(hf, r2)


def _row_copy(src_hbm, row, dst, slot, sem):
    return pltpu.make_async_copy(src_hbm.at[pl.ds(row, 1), :], dst.at[pl.ds(slot, 1), :], sem)


ROW_COPY_UNROLL = 8


def _issue_rows(src_hbm, idx_ref, dst, sem, tm):
    def issue(r, carry):
        _row_copy(src_hbm, idx_ref[0, r], dst, r, sem).start()
        return carry

    lax.fori_loop(0, tm, issue, 0, unroll=ROW_COPY_UNROLL)


def _wait_rows(src_hbm, dst, sem, tm):
    pltpu.make_async_copy(src_hbm.at[pl.ds(0, tm), :], dst, sem).wait()


def _gather_kernel(nt_ref, src_ref, src_next_ref, h_hbm, o_ref, buf, sem, *, tm):
    t = pl.program_id(0)
    nt = nt_ref[0]
    slot = t % 2

    @pl.when(t == 0)
    def _():
        _issue_rows(h_hbm, src_ref, buf.at[0], sem.at[0], tm)

    @pl.when(t + 1 < nt)
    def _():
        _issue_rows(h_hbm, src_next_ref, buf.at[1 - slot], sem.at[1 - slot], tm)

    @pl.when(t < nt)
    def _():
        _wait_rows(h_hbm, buf.at[slot], sem.at[slot], tm)
        o_ref[...] = buf[slot].astype(o_ref.dtype)

    @pl.when(t >= nt)
    def _():
        o_ref[...] = jnp.zeros_like(o_ref)


def _gather_call(dm, hf, src_tok, n_tiles, tm, nt_max):
    src3 = src_tok.reshape(nt_max, 1, tm)
    grid_spec = pltpu.PrefetchScalarGridSpec(
        num_scalar_prefetch=1, grid=(nt_max,),
        in_specs=[pl.BlockSpec((None, 1, tm), lambda t, nt: (t, 0, 0), memory_space=pltpu.SMEM),
                  pl.BlockSpec((None, 1, tm), lambda t, nt: (jnp.minimum(t + 1, nt_max - 1), 0, 0),
                               memory_space=pltpu.SMEM),
                  pl.BlockSpec(memory_space=pl.ANY)],
        out_specs=pl.BlockSpec((tm, dm.D), lambda t, nt: (t, 0)),
        scratch_shapes=[pltpu.VMEM((2, tm, dm.D), F32), pltpu.SemaphoreType.DMA((2,))])
    est = 2 * tm * dm.D * 4 + 2 * tm * dm.D * 2 + 2 * tm * dm.D * 4
    return pl.pallas_call(
        functools.partial(_gather_kernel, tm=tm), grid_spec=grid_spec,
        out_shape=jax.ShapeDtypeStruct((nt_max * tm, dm.D), BF16),
        compiler_params=_params(("arbitrary",), est),
        name="moe_gather",
    )(n_tiles, src3, src3, hf)


def _combine_kernel(p1_ref, p2_ref, p1n_ref, p2n_ref, y_hbm, g1_ref, g2_ref, x_ref, gate_ref, gpost_ref, o_ref,
                    buf, sem, *, tm, n_blocks):
    t = pl.program_id(0)
    slot = t % 2

    def issue(first_ref, second_ref, s):
        _issue_rows(y_hbm, first_ref, buf.at[s, 0], sem.at[s], tm)
        _issue_rows(y_hbm, second_ref, buf.at[s, 1], sem.at[s], tm)

    @pl.when(t == 0)
    def _():
        issue(p1_ref, p2_ref, 0)

    @pl.when(t + 1 < n_blocks)
    def _():
        issue(p1n_ref, p2n_ref, 1 - slot)

    for k in range(2):
        _wait_rows(y_hbm, buf.at[slot, k], sem.at[slot], tm)
    f = g1_ref[...] * buf[slot, 0] + g2_ref[...] * buf[slot, 1]
    o_ref[...] = x_ref[...] + gate_ref[...] * _rms(f, gpost_ref[...])


def _combine_call(dm, y_sorted, pos1, pos2, g1, g2, x, mod3, gate_lk, g_post, tm):
    d = dm.D
    nblk = dm.NL // tm
    layer, k = gate_lk
    idx_spec = pl.BlockSpec((None, 1, tm), lambda i: (i, 0, 0), memory_space=pltpu.SMEM)
    idx_next = pl.BlockSpec((None, 1, tm), lambda i: (jnp.minimum(i + 1, nblk - 1), 0, 0),
                            memory_space=pltpu.SMEM)
    col_spec = pl.BlockSpec((tm, 1), lambda i: (i, 0))
    est = 4 * tm * d * 4 + 4 * tm * d * 4 + 4 * tm * d * 4 + 4 * tm * LANES * 4
    p1, p2 = pos1.reshape(nblk, 1, tm), pos2.reshape(nblk, 1, tm)
    return pl.pallas_call(
        functools.partial(_combine_kernel, tm=tm, n_blocks=nblk),
        grid=(nblk,),
        in_specs=[idx_spec, idx_spec, idx_next, idx_next, pl.BlockSpec(memory_space=pl.ANY), col_spec, col_spec,
                  pl.BlockSpec((tm, d), lambda i: (i, 0)),
                  pl.BlockSpec((None, 1, d),
                               lambda i: ((layer * MOD_ROWS + (i * tm) // dm.S) * N_MOD + k, 0, 0)),
                  pl.BlockSpec((1, d), lambda i: (0, 0))],
        out_specs=pl.BlockSpec((tm, d), lambda i: (i, 0)),
        out_shape=jax.ShapeDtypeStruct((dm.NL, d), F32),
        scratch_shapes=[pltpu.VMEM((2, 2, tm, d), F32), pltpu.SemaphoreType.DMA((2,))],
        compiler_params=_params(("arbitrary",), est),
        name="moe_combine_residual",
    )(p1, p2, p1, p2, y_sorted, g1.reshape(dm.NL, 1), g2.reshape(dm.NL, 1), x, mod3, g_post.reshape(1, d))


def _moe_plan(dm, rout, tm):
    t = dm.NL
    e_flat = jnp.concatenate([rout[:, 0], rout[:, 1]]).astype(jnp.int32)
    onehot = (e_flat[:, None] == jnp.arange(dm.E, dtype=jnp.int32)[None, :]).astype(jnp.int32)
    before = jnp.cumsum(onehot, axis=0) - onehot
    rank = jnp.sum(before * onehot, axis=1)
    counts = jnp.sum(onehot, axis=0)
    tiles = (counts + tm - 1) // tm
    tile_end = jnp.cumsum(tiles)
    tile_start = tile_end - tiles
    dest = jnp.sum(onehot * tile_start[None, :], axis=1) * tm + rank
    nt_max = (2 * t) // tm + dm.E
    n_tiles = tile_end[-1:].astype(jnp.int32)
    tile_ids = jnp.arange(nt_max, dtype=jnp.int32)
    tile_expert = jnp.minimum(jnp.sum((tile_ids[:, None] >= tile_end[None, :]).astype(jnp.int32), axis=1),
                              dm.E - 1).astype(jnp.int32)
    ids = jnp.arange(dm.E, dtype=jnp.int32)
    later = jnp.logical_and(tiles[None, :] > 0, ids[None, :] > ids[:, None])
    next_of = jnp.min(jnp.where(later, ids[None, :], dm.E), axis=1)
    next_of = jnp.where(next_of == dm.E, -1, next_of).astype(jnp.int32)
    next_expert = jnp.sum(jnp.where(tile_expert[:, None] == ids[None, :], next_of[None, :], 0), axis=1)
    tok = jnp.concatenate([jnp.arange(t, dtype=jnp.int32)] * 2)
    src_tok = jnp.zeros((nt_max * tm,), jnp.int32).at[dest].set(tok)
    return src_tok, dest[:t], dest[t:], (tile_expert, next_expert.astype(jnp.int32), n_tiles), nt_max


def _rope_tables(dm):
    half = HEAD_DIM // 2
    freqs = ROPE_THETA ** (-jnp.arange(0, half, 2, dtype=F32) / half)
    pos = jnp.arange(dm.S, dtype=jnp.int32)
    row = (pos // dm.GW).astype(F32)[:, None] * freqs
    col = (pos % dm.GW).astype(F32)[:, None] * freqs
    cos = jnp.concatenate([jnp.cos(row), jnp.cos(row), jnp.cos(col), jnp.cos(col)], axis=1)
    sin = jnp.concatenate([-jnp.sin(row), jnp.sin(row), -jnp.sin(col), jnp.sin(col)], axis=1)
    cos = jnp.concatenate([jnp.tile(cos, (dm.B, 1)), jnp.ones((dm.NC, HEAD_DIM), F32)], axis=0)
    sin = jnp.concatenate([jnp.tile(sin, (dm.B, 1)), jnp.zeros((dm.NC, HEAD_DIM), F32)], axis=0)
    return cos, sin


def _trunk(dm, x, c, ctx, c_ctx, ada_w, ada_b, norm_mix_pre, norm_mix_post, norm_ffn_pre, norm_ffn_post,
           w_in, attn_sink, attn_norm, conv_w, conv_b, dt_bias, a_log, d_skip, ssm_norm, w_out,
           ffn_w_gate, ffn_w_up, ffn_w_down, moe_router, moe_w_gate, moe_w_up, moe_w_down):
    depth = ada_w.shape[0]
    d = dm.D
    x2 = x.reshape(dm.NL, d)
    c2 = ctx.reshape(dm.NC, d)
    cvec = jnp.concatenate([c, c_ctx[None, :], jnp.zeros((MOD_ROWS - dm.B - 1, d), F32)], axis=0)
    mod3 = _mod_call(cvec, ada_w, ada_b).reshape(depth * MOD_ROWS * N_MOD, 1, d)
    cos, sin = _rope_tables(dm)

    tm_all = _row_tile(dm.M, MM_ROW_TILE_CAP)
    tm_lat = _row_tile(dm.NL, MM_ROW_TILE_CAP)
    tn_in = _pick(math.gcd(math.gcd(dm.AD, dm.SD), math.gcd(dm.KD, dm.CD)), (512, 256, 128))
    nq, nz, nk, nx = dm.AD // tn_in, dm.SD // tn_in, dm.KD // tn_in, dm.CD // tn_in
    src_qk = lambda j: jnp.where(j < nq, j, j + nz)
    src_zxv = lambda j: jnp.where(j < nz, j + nq,
                                  jnp.where(j < nz + nx, j + nq + 2 * nk, j - nx + nq + nk))
    w_in_t = jnp.swapaxes(w_in, 1, 2)
    w_dt = w_in_t[:, dm.DT0:, :].reshape(depth, N_DIR, dm.SH, d)
    w_dt = jnp.pad(w_dt, ((0, 0), (0, 0), (0, LANES - dm.SH), (0, 0))).reshape(depth, N_DIR * LANES, d)

    xa = None
    for i in range(depth):
        last = i == depth - 1
        if i == 0:
            _, h = _norm_call(dm, dm.M, x2, c2, None, None, None, norm_mix_pre[i], (i, 0), (i, 1), mod3, BF16)
        qk = _mm_call([h], [w_in_t], i, rows=dm.M, tm=tm_all, tn=tn_in, n_cols=dm.AD + dm.KD, out_dtype=BF16,
                      rope=(cos, sin), w_nt=True, panel_src=src_qk, name="proj_in_qk")
        big = _mm_call([h], [w_in_t], i, rows=dm.M, tm=tm_all, tn=tn_in, n_cols=dm.SD + dm.CD + dm.KD,
                       out_dtype=BF16, w_nt=True, panel_src=src_zxv, name="proj_in_zxv")
        dt_pad = _mm_call([h], [w_dt], i, rows=dm.M, tm=tm_all, tn=N_DIR * LANES, n_cols=N_DIR * LANES,
                          out_dtype=F32, w_nt=True, name="proj_dt")
        u = _conv_call(dm, big, conv_w[i], conv_b[i])
        y2 = _ssd_call(dm, u, dt_pad, dt_bias[i], a_log[i], d_skip[i])
        rows = dm.NL if last else dm.M
        tm_rows = tm_lat if last else tm_all
        ssm = _gate_call(dm, rows, y2, big, ssm_norm[i])
        attn = _attn_call(dm, qk, big, attn_sink[i], attn_norm[i], with_ctx=not last)
        y = _mm_call([attn, ssm], [w_out], i, rows=rows, tm=tm_rows, tn=_pick(d, (512, 256, 128)), n_cols=d,
                     out_dtype=BF16, name="proj_out")
        if i == 0:
            xa, hf = _norm_call(dm, rows, x2, c2, y, norm_mix_post[i], (i, 2), norm_ffn_pre[i], (i, 3), (i, 4),
                                mod3, BF16 if i % 2 == 0 else F32)
        else:
            xa, hf = _norm_call(dm, rows, xa, None, y, norm_mix_post[i], (i, 2), norm_ffn_pre[i], (i, 3), (i, 4),
                                mod3, BF16 if i % 2 == 0 else F32)
        j = i // 2
        if i % 2 == 0:
            act = _mm_call([hf], [ffn_w_gate, ffn_w_up], j, rows=rows, tm=tm_rows,
                           tn=_pick(dm.DFF, (256, 128)), n_cols=dm.DFF, out_dtype=BF16, mode="swiglu",
                           name="ffn_gate_up")
            f = _mm_call([act], [ffn_w_down], j, rows=rows, tm=_row_tile(rows, MM_ROW_TILE_CAP // 4),
                         tn=_pick(d, (512, 256, 128)), n_cols=d, out_dtype=BF16, name="ffn_down")
            if last:
                raise NotImplementedError("dense FFN on the last layer")
            xa, h = _norm_call(dm, rows, xa, None, f, norm_ffn_post[i], (i, 5), norm_mix_pre[i + 1],
                               (i + 1, 0), (i + 1, 1), mod3, BF16)
        else:
            if not last:
                raise NotImplementedError("MoE FFN on a non-final layer")
            tm_e = _pick(dm.NL, (256, 128))
            rout = _router_call(dm, hf, moe_router[j])
            src_tok, pos1, pos2, tables, nt_max = _moe_plan(dm, rout, tm_e)
            xs = _gather_call(dm, hf, src_tok, tables[2], tm_e, nt_max)
            n_exp = moe_w_gate.shape[1]
            wg = moe_w_gate.reshape((-1,) + moe_w_gate.shape[2:])
            wu = moe_w_up.reshape((-1,) + moe_w_up.shape[2:])
            wd = moe_w_down.reshape((-1,) + moe_w_down.shape[2:])
            act = _mm_call([xs], [wg, wu], j * n_exp, rows=nt_max * tm_e, tm=tm_e,
                           tn=_pick(dm.DE, (512, 256, 128)), n_cols=dm.DE, out_dtype=BF16, mode="swiglu",
                           grouped=tables, name="moe_gate_up")
            ys = _mm_call([act], [wd], j * n_exp, rows=nt_max * tm_e, tm=tm_e,
                          tn=_pick(d, (1024, 512, 256, 128)), n_cols=d, out_dtype=F32,
                          grouped=tables, name="moe_down")
            xa = _combine_call(dm, ys, pos1, pos2, rout[:, 2], rout[:, 3], xa, mod3, (i, 5), norm_ffn_post[i],
                               tm_e)
    return xa[:dm.NL].reshape(dm.B, dm.S, d)


def kernel(x, c, ctx, c_ctx, ada_w, ada_b, norm_mix_pre, norm_mix_post, norm_ffn_pre, norm_ffn_post, w_in,
           attn_sink, attn_norm, conv_w, conv_b, dt_bias, a_log, d_skip, ssm_norm, w_out, ffn_w_gate, ffn_w_up,
           ffn_w_down, moe_router, moe_w_gate, moe_w_up, moe_w_down):
    b, s, d = x.shape
    sh = a_log.shape[2]
    cd = conv_w.shape[2]
    sd = sh * SSM_HEAD_DIM
    dm = Dims(B=b, S=s, L=ctx.shape[1], D=d, AH=attn_sink.shape[1], KVH=ATTN_KV_HEADS, SH=sh,
              SG=(cd - sd) // (2 * SSM_STATE), DFF=ffn_w_gate.shape[2], E=moe_router.shape[2],
              DE=moe_w_gate.shape[3], GW=GRID_W)
    assert w_in.shape[2] == dm.DT0 + N_DIR * dm.SH and w_out.shape[1] == dm.AD + dm.SD and dm.AD == dm.SD
    assert dm.S % BLOCK == 0 and dm.L % BLOCK == 0 and dm.SH <= LANES and dm.E <= LANES
    return _trunk(dm, x, c, ctx, c_ctx, ada_w, ada_b, norm_mix_pre, norm_mix_post, norm_ffn_pre, norm_ffn_post,
                  w_in, attn_sink, attn_norm, conv_w, conv_b, dt_bias, a_log, d_skip, ssm_norm, w_out,
                  ffn_w_gate, ffn_w_up, ffn_w_down, moe_router, moe_w_gate, moe_w_up, moe_w_down)
```

```python
import functools
import math
from typing import NamedTuple

import jax
import jax.numpy as jnp
from jax import lax
from jax.experimental import pallas as pl
from jax.experimental.pallas import tpu as pltpu

F32 = jnp.float32
BF16 = jnp.bfloat16

HEAD_DIM = 128
WINDOW = 128
BLOCK = 128
ROPE_THETA = 10000.0
SSM_HEAD_DIM = 64
SSM_STATE = 128
CONV_K = 5
CHUNK = 128
N_DIR = 2
GRID_W = 64
ATTN_KV_HEADS = 4
N_MOD = 6
EPS = 1e-6
MOD_ROWS = 8

V7X_VMEM_BYTES = 64 * 1024 * 1024
VMEM_LIMIT_CAP = 56 * 1024 * 1024
LANES = 128
BF16_SUBLANES = 16
MM_ROW_TILE_CAP = 1100
MM_ROW_CHUNKS = 4
MM_MIN_CHUNK_ROWS = 256


class Dims(NamedTuple):
    B: int
    S: int
    L: int
    D: int
    AH: int
    KVH: int
    SH: int
    SG: int
    DFF: int
    E: int
    DE: int
    GW: int

    @property
    def AD(self):
        return self.AH * HEAD_DIM

    @property
    def KD(self):
        return self.KVH * HEAD_DIM

    @property
    def SD(self):
        return self.SH * SSM_HEAD_DIM

    @property
    def BC(self):
        return self.SG * SSM_STATE

    @property
    def CD(self):
        return self.SD + 2 * self.BC

    @property
    def K0(self):
        return self.AD + self.SD

    @property
    def V0(self):
        return self.K0 + self.KD

    @property
    def X0(self):
        return self.V0 + self.KD

    @property
    def DT0(self):
        return self.X0 + self.CD

    @property
    def NL(self):
        return self.B * self.S

    @property
    def NC(self):
        return self.B * self.L

    @property
    def M(self):
        return self.NL + self.NC


def _pick(n, prefs):
    for p in prefs:
        if n % p == 0:
            return p
    raise ValueError(f"no tile in {prefs} divides {n}")


def _row_tile(rows, cap):
    for t in range(min(cap, rows), 0, -1):
        if rows % t == 0 and t % BF16_SUBLANES == 0:
            return t
    raise ValueError(f"no row tile <= {cap} divides {rows}")


def _row_chunks(tm, want):
    for c in range(want, 0, -1):
        if tm % c == 0 and (tm // c) % BF16_SUBLANES == 0 and tm // c >= MM_MIN_CHUNK_ROWS:
            return c
    return 1


def _params(sem, est_bytes):
    limit = int(min(max(est_bytes * 5 // 4 + (2 << 20), 16 << 20), VMEM_LIMIT_CAP))
    return pltpu.CompilerParams(dimension_semantics=sem, vmem_limit_bytes=limit)


def _silu(v):
    return v / (1.0 + jnp.exp(-v))


def _rms(v, g):
    return v * lax.rsqrt(jnp.mean(v * v, axis=-1, keepdims=True) + EPS) * g


def _mod_kernel(c_ref, w_ref, b_ref, o_ref):
    c = _silu(c_ref[...]).astype(BF16)
    o_ref[...] = jnp.dot(c, w_ref[...].astype(BF16), preferred_element_type=F32) + b_ref[...]


def _mod_call(cvec, ada_w, ada_b):
    depth, d, n = ada_w.shape
    tn = _pick(n, (512, 256, 128))
    est = 2 * d * tn * 4 + d * tn * 2 + 4 * MOD_ROWS * tn * 4
    return pl.pallas_call(
        _mod_kernel,
        grid=(depth, n // tn),
        in_specs=[
            pl.BlockSpec((MOD_ROWS, d), lambda l, j: (0, 0)),
            pl.BlockSpec((None, d, tn), lambda l, j: (l, 0, j)),
            pl.BlockSpec((None, 1, tn), lambda l, j: (l, 0, j)),
        ],
        out_specs=pl.BlockSpec((None, MOD_ROWS, tn), lambda l, j: (l, 0, j)),
        out_shape=jax.ShapeDtypeStruct((depth, MOD_ROWS, n), F32),
        compiler_params=_params(("arbitrary", "arbitrary"), est),
        name="adaln_mod",
    )(cvec, ada_w, ada_b.reshape(depth, 1, n))


def _norm_kernel(*refs, split, has_res, n_lat_blocks):
    refs = list(refs)
    x_ref = refs.pop(0)
    c_ref = refs.pop(0) if split else None
    if has_res:
        y_ref, gpost_ref, gate_ref = refs.pop(0), refs.pop(0), refs.pop(0)
    gpre_ref, sh_ref, sc_ref = refs.pop(0), refs.pop(0), refs.pop(0)
    if has_res:
        xo_ref = refs.pop(0)
    h_ref = refs.pop(0)

    def run(src_ref):
        x = src_ref[...]
        if has_res:
            x = x + gate_ref[...] * _rms(y_ref[...].astype(F32), gpost_ref[...])
            xo_ref[...] = x
        h_ref[...] = (_rms(x, gpre_ref[...]) * (1.0 + sc_ref[...]) + sh_ref[...]).astype(h_ref.dtype)

    if split:
        i = pl.program_id(0)
        pl.when(i < n_lat_blocks)(lambda: run(x_ref))
        pl.when(i >= n_lat_blocks)(lambda: run(c_ref))
    else:
        run(x_ref)


def _norm_call(dm, rows, x, ctx2d, y, g_post, gate_lk, g_pre, sh_lk, sc_lk, mod3, h_dtype):
    d = dm.D
    tm = _pick(dm.L, (256, 128))
    nblk = rows // tm
    nlat = dm.NL // tm
    split = ctx2d is not None
    has_res = y is not None

    def mod_spec(lk):
        layer, k = lk
        return pl.BlockSpec(
            (None, 1, d),
            lambda i: ((layer * MOD_ROWS + jnp.minimum((i * tm) // dm.S, dm.B)) * N_MOD + k, 0, 0))

    row_spec = pl.BlockSpec((tm, d), lambda i: (i, 0))
    vec_spec = pl.BlockSpec((1, d), lambda i: (0, 0))
    in_specs, args = [], []
    if split:
        in_specs += [pl.BlockSpec((tm, d), lambda i: (jnp.minimum(i, nlat - 1), 0)),
                     pl.BlockSpec((tm, d), lambda i: (jnp.maximum(i - nlat, 0), 0))]
        args += [x, ctx2d]
    else:
        in_specs += [row_spec]
        args += [x]
    if has_res:
        in_specs += [row_spec, vec_spec, mod_spec(gate_lk)]
        args += [y, g_post.reshape(1, d), mod3]
    in_specs += [vec_spec, mod_spec(sh_lk), mod_spec(sc_lk)]
    args += [g_pre.reshape(1, d), mod3, mod3]
    out_specs, out_shape = [], []
    if has_res:
        out_specs.append(row_spec)
        out_shape.append(jax.ShapeDtypeStruct((rows, d), F32))
    out_specs.append(row_spec)
    out_shape.append(jax.ShapeDtypeStruct((rows, d), h_dtype))
    est = 2 * tm * d * (4 + 4 + 2 + 4 + 4)
    outs = pl.pallas_call(
        functools.partial(_norm_kernel, split=split, has_res=has_res, n_lat_blocks=nlat),
        grid=(nblk,),
        in_specs=in_specs,
        out_specs=out_specs,
        out_shape=out_shape,
        compiler_params=_params(("arbitrary",), est),
        name="residual_prenorm" if has_res else "prenorm",
    )(*args)
    return outs if has_res else (None, outs[0])


def _rope(acc, cos, sin_signed, tn):
    lane = lax.broadcasted_iota(jnp.int32, acc.shape, 1)
    first = (lane % 64) < 32
    partner = jnp.where(first, pltpu.roll(acc, tn - 32, 1), pltpu.roll(acc, 32, 1))
    reps = tn // HEAD_DIM
    cos_t = jnp.concatenate([cos] * reps, axis=1) if reps > 1 else cos
    sin_t = jnp.concatenate([sin_signed] * reps, axis=1) if reps > 1 else sin_signed
    return acc * cos_t + partner * sin_t


def _mm_kernel(*refs, n_lhs, k_sizes, n_w, mode, with_rope, grouped, tn, w_nt, w_group, n_panels, panel_src,
               row_chunks):
    refs = list(refs)
    if grouped:
        te_ref, nxt_ref, nt_ref = refs.pop(0), refs.pop(0), refs.pop(0)
    lhs_refs = [refs.pop(0) for _ in range(n_lhs)]
    w_refs = [refs.pop(0) for _ in range(n_w)]
    if with_rope:
        cos_ref, sin_ref = refs.pop(0), refs.pop(0)
    o_ref = refs.pop(0)
    wb_refs = [refs.pop(0) for _ in range(n_w)]
    stage_refs = [refs.pop(0) for _ in range(n_w)]
    sem = refs.pop(0)
    j = pl.program_id(0)
    m = pl.program_id(1)

    def panel_copy(i, g, jp):
        col = pl.multiple_of(panel_src(jp) * tn, tn)
        src = w_refs[i].at[g, pl.ds(col, tn), :] if w_nt else w_refs[i].at[g, :, pl.ds(col, tn)]
        return pltpu.make_async_copy(src, stage_refs[i], sem.at[i])

    if grouped:
        nt = nt_ref[0]
        active = m < nt
        new_panel = jnp.logical_or(m == 0, jnp.logical_and(active, te_ref[m] != te_ref[jnp.maximum(m - 1, 0)]))
        g_cur = w_group + te_ref[m]
        more_here = nxt_ref[m] >= 0
        g_next = w_group + jnp.where(more_here, nxt_ref[m], te_ref[0])
        j_next = jnp.where(more_here, j, j + 1)
        has_next = jnp.logical_or(more_here, j + 1 < n_panels)
    else:
        active = None
        new_panel = m == 0
        g_cur = g_next = w_group
        j_next = j + 1
        has_next = j + 1 < n_panels

    @pl.when(jnp.logical_and(j == 0, m == 0))
    def _():
        for i in range(n_w):
            panel_copy(i, g_cur, j).start()

    @pl.when(new_panel)
    def _():
        for i in range(n_w):
            panel_copy(i, g_cur, j).wait()
            wb_refs[i][...] = stage_refs[i][...].astype(BF16)

        @pl.when(has_next)
        def _():
            for i in range(n_w):
                panel_copy(i, g_next, j_next).start(priority=1)

    def compute():
        rc = o_ref.shape[0] // row_chunks
        for c in range(row_chunks):
            rows = slice(c * rc, (c + 1) * rc)
            accs = []
            for wb_ref in wb_refs:
                acc, k0 = None, 0
                for l_ref, ks in zip(lhs_refs, k_sizes):
                    if w_nt:
                        part = lax.dot_general(l_ref[rows, :], wb_ref[:, k0:k0 + ks], (((1,), (1,)), ((), ())),
                                               preferred_element_type=F32)
                    else:
                        part = jnp.dot(l_ref[rows, :], wb_ref[k0:k0 + ks, :], preferred_element_type=F32)
                    acc = part if acc is None else acc + part
                    k0 += ks
                accs.append(acc)
            res = _silu(accs[0]) * accs[1] if mode == "swiglu" else accs[0]
            if with_rope:
                res = _rope(res, cos_ref[rows, :], sin_ref[rows, :], tn)
            o_ref[rows, :] = res.astype(o_ref.dtype)

    if grouped:
        pl.when(active)(compute)

        @pl.when(jnp.logical_not(active))
        def _():
            o_ref[...] = jnp.zeros_like(o_ref)
    else:
        compute()


def _mm_call(lhs_list, w_list, w_group, *, rows, tm, tn, n_cols, out_dtype, mode="plain",
             rope=None, grouped=None, w_nt=False, panel_src=None, name="panel_matmul"):
    k_sizes = tuple(a.shape[1] for a in lhs_list)
    k_tot = sum(k_sizes)
    n_w = len(w_list)
    assert all(w.shape[2 if w_nt else 1] == k_tot for w in w_list)
    n_panels = n_cols // tn
    n_blocks = rows // tm
    args, in_specs = [], []
    src = panel_src if panel_src is not None else (lambda j: j)
    if grouped is not None:
        lhs_map = lambda j, m, te, nx, nt: (jnp.minimum(m, nt[0] - 1), 0)
        out_map = lambda j, m, te, nx, nt: (m, j)
    else:
        lhs_map = lambda j, m: (m, 0)
        out_map = lambda j, m: (m, j)
    for a, ks in zip(lhs_list, k_sizes):
        in_specs.append(pl.BlockSpec((tm, ks), lhs_map))
        args.append(a)
    w_block = (tn, k_tot) if w_nt else (k_tot, tn)
    for w in w_list:
        in_specs.append(pl.BlockSpec(memory_space=pl.ANY))
        args.append(w)
    if rope is not None:
        cos, sin_signed = rope
        in_specs += [pl.BlockSpec((tm, HEAD_DIM), lambda j, m: (m, 0))] * 2
        args += [cos, sin_signed]
    out_bytes = jnp.dtype(out_dtype).itemsize
    est = (n_w * k_tot * tn * (4 + 2) + 2 * tm * k_tot * 2 + 2 * tm * tn * out_bytes + (2 + n_w) * tm * tn * 4)
    kern = functools.partial(_mm_kernel, n_lhs=len(lhs_list), k_sizes=k_sizes, n_w=n_w, mode=mode,
                             with_rope=rope is not None, grouped=grouped is not None, tn=tn, w_nt=w_nt,
                             w_group=w_group, n_panels=n_panels, panel_src=src,
                             row_chunks=_row_chunks(tm, MM_ROW_CHUNKS))
    scratch = ([pltpu.VMEM(w_block, BF16) for _ in range(n_w)] + [pltpu.VMEM(w_block, F32) for _ in range(n_w)]
               + [pltpu.SemaphoreType.DMA((n_w,))])
    out_shape = jax.ShapeDtypeStruct((rows, n_cols), out_dtype)
    cp = _params(("arbitrary", "arbitrary"), est)
    if grouped is not None:
        grid_spec = pltpu.PrefetchScalarGridSpec(
            num_scalar_prefetch=3, grid=(n_panels, n_blocks), in_specs=in_specs,
            out_specs=pl.BlockSpec((tm, tn), out_map), scratch_shapes=scratch)
        return pl.pallas_call(kern, grid_spec=grid_spec, out_shape=out_shape, compiler_params=cp,
                              name=name)(*grouped, *args)
    return pl.pallas_call(kern, grid=(n_panels, n_blocks), in_specs=in_specs,
                          out_specs=pl.BlockSpec((tm, tn), out_map), out_shape=out_shape,
                          scratch_shapes=scratch, compiler_params=cp, name=name)(*args)


def _conv_kernel(prev_ref, cur_ref, next_ref, w_ref, b_ref, o_ref, *, tb, seg_lat, seg_ctx, n_lat_rows):
    i = pl.program_id(0)
    row0 = i * tb
    in_lat = row0 < n_lat_rows
    off = jnp.where(in_lat, row0 % seg_lat, (row0 - n_lat_rows) % seg_ctx)
    seg = jnp.where(in_lat, seg_lat, seg_ctx)
    keep_prev = jnp.where(off != 0, 1.0, 0.0)
    keep_next = jnp.where(off + tb != seg, 1.0, 0.0)
    halo = BF16_SUBLANES
    prev = prev_ref[...].astype(F32)[halo - 8:, :] * keep_prev
    nxt = next_ref[...].astype(F32)[:8, :] * keep_next
    ext = jnp.concatenate([prev, cur_ref[...].astype(F32), nxt], axis=0)
    rows = tb + 16
    pad = CONV_K // 2
    acc = None
    for k in range(CONV_K):
        shift = (pad - k) % rows
        shifted = ext if shift == 0 else pltpu.roll(ext, shift, 0)
        term = shifted[8:8 + tb, :] * w_ref[k:k + 1, :]
        acc = term if acc is None else acc + term
    o_ref[...] = _silu(acc + b_ref[...]).astype(o_ref.dtype)


def _conv_call(dm, big, conv_w, conv_b):
    tb = _pick(dm.L, (256, 128))
    cw = _pick(math.gcd(dm.SD, dm.CD), (1024, 512, 256, 128))
    halo = BF16_SUBLANES
    col0 = dm.SD // cw
    nrb = dm.M // tb
    last_halo = dm.M // halo - 1
    est = 2 * (tb + 2 * halo) * cw * 2 + 2 * tb * cw * 2 + 8 * (tb + 16) * cw * 4
    return pl.pallas_call(
        functools.partial(_conv_kernel, tb=tb, seg_lat=dm.S, seg_ctx=dm.L, n_lat_rows=dm.NL),
        grid=(nrb, dm.CD // cw),
        in_specs=[
            pl.BlockSpec((halo, cw), lambda i, j: (jnp.maximum(i * (tb // halo) - 1, 0), col0 + j)),
            pl.BlockSpec((tb, cw), lambda i, j: (i, col0 + j)),
            pl.BlockSpec((halo, cw), lambda i, j: (jnp.minimum((i + 1) * (tb // halo), last_halo), col0 + j)),
            pl.BlockSpec((8, cw), lambda i, j: (0, j)),
            pl.BlockSpec((1, cw), lambda i, j: (0, j)),
        ],
        out_specs=pl.BlockSpec((tb, cw), lambda i, j: (i, j)),
        out_shape=jax.ShapeDtypeStruct((dm.M, dm.CD), BF16),
        compiler_params=_params(("arbitrary", "arbitrary"), est),
        name="dwconv_silu",
    )(big, big, big, jnp.pad(conv_w, ((0, 8 - CONV_K), (0, 0))), conv_b.reshape(1, dm.CD))


def _split3(v):
    hi = v.astype(BF16)
    r1 = v - hi.astype(F32)
    mid = r1.astype(BF16)
    lo = (r1 - mid.astype(F32)).astype(BF16)
    return hi, mid, lo


def _softplus(v):
    t = jnp.exp(-jnp.abs(v))
    y = 1.0 + t
    return jnp.maximum(v, 0.0) + (jnp.log(y) - ((y - 1.0) - t) / y)


def _ssd_kernel(u_ref, dt_ref, bias_ref, alog_ref, dsk_ref, y_ref, st_ref, *, SH, SG):
    d = pl.program_id(1)
    s = pl.program_id(2)
    hpg = SH // SG
    sd = SH * SSM_HEAD_DIM
    bc = SG * SSM_STATE
    c = CHUNK

    @pl.when(s == 0)
    def _():
        st_ref[...] = jnp.zeros_like(st_ref)

    dt = _softplus(dt_ref[...] + bias_ref[...])
    a = dt * (-jnp.exp(alog_ref[...]))
    row = lax.broadcasted_iota(jnp.int32, (c, c), 0)
    col = lax.broadcasted_iota(jnp.int32, (c, c), 1)
    fwd = d == 0
    tri = jnp.where(fwd, col, row) <= jnp.where(fwd, row, col)
    hi, mid, lo = _split3(a)
    tri_b = jnp.where(tri, 1.0, 0.0).astype(BF16)
    parts = jnp.dot(tri_b, jnp.concatenate([hi, mid, lo], axis=1), preferred_element_type=F32)
    cum = parts[:, :LANES] + parts[:, LANES:2 * LANES] + parts[:, 2 * LANES:]
    tot = jnp.sum(a, axis=0, keepdims=True)
    w_t = (jnp.exp(tot - cum) * dt).T
    cum_t = cum.T
    dt_t = dt.T
    etot = jnp.exp(tot)
    lane_lo = lax.broadcasted_iota(jnp.int32, (c, LANES), 1) < SSM_HEAD_DIM
    lane_lo1 = lax.broadcasted_iota(jnp.int32, (1, LANES), 1) < SSM_HEAD_DIM
    neg_inf = jnp.float32(-jnp.inf)

    for g in range(SG):
        b_g = u_ref[:, sd + g * SSM_STATE: sd + (g + 1) * SSM_STATE]
        c_g = u_ref[:, sd + bc + g * SSM_STATE: sd + bc + (g + 1) * SSM_STATE]
        cb = lax.dot_general(c_g, b_g, (((1,), (1,)), ((), ())), preferred_element_type=F32)
        bt_g = b_g.astype(F32).T
        c_g32 = c_g.astype(F32)
        for pr in range(hpg // 2):
            h0 = g * hpg + 2 * pr
            lo_col, hi_col = h0 * SSM_HEAD_DIM, (h0 + 2) * SSM_HEAD_DIM
            xs_pair = u_ref[:, lo_col:hi_col]
            st_pair = st_ref[:, lo_col:hi_col]
            rhs = jnp.concatenate([xs_pair, st_pair.astype(BF16)], axis=0)
            ys, sts, ets = [], [], []
            for q in range(2):
                h = h0 + q
                colb = jnp.broadcast_to(cum[:, h:h + 1], (c, c))
                decay = jnp.exp(jnp.where(tri, colb - cum_t[h:h + 1, :], neg_inf))
                m_h = decay * cb * dt_t[h:h + 1, :]
                c_h = c_g32 * jnp.exp(colb)
                lhs = jnp.concatenate([m_h, c_h], axis=1).astype(BF16)
                ys.append(jnp.dot(lhs, rhs, preferred_element_type=F32))
                btw = (bt_g * w_t[h:h + 1, :]).astype(BF16)
                sts.append(jnp.dot(btw, xs_pair, preferred_element_type=F32))
                ets.append(jnp.broadcast_to(etot[:, h:h + 1], (1, LANES)))
            y_pair = jnp.where(lane_lo, ys[0], ys[1]) + dsk_ref[:, lo_col:hi_col] * xs_pair.astype(F32)
            e_pair = jnp.where(lane_lo1, ets[0], ets[1])
            st_ref[:, lo_col:hi_col] = e_pair * st_pair + jnp.where(lane_lo, sts[0], sts[1])
            y_ref[:, lo_col:hi_col] = y_pair.astype(y_ref.dtype)


def _ssd_call(dm, u, dt_pad, dt_bias, a_log, d_skip):
    nl, ns = dm.L // CHUNK, dm.S // CHUNK
    lat_blocks = dm.NL // CHUNK

    def rb(b, d, s):
        ctx_chunk = jnp.where(d == 0, s, nl - 1 - s)
        lat_chunk = jnp.where(d == 0, s - nl, ns - 1 - (s - nl))
        return jnp.where(s < nl, lat_blocks + b * nl + ctx_chunk, b * ns + lat_chunk)

    pad = LANES - dm.SH
    bias_p = jnp.pad(dt_bias, ((0, 0), (0, pad))).reshape(N_DIR, 1, LANES)
    alog_p = jnp.pad(a_log, ((0, 0), (0, pad))).reshape(N_DIR, 1, LANES)
    dsk = jnp.repeat(d_skip, SSM_HEAD_DIM)
    dsk_p = jnp.stack([dsk, jnp.zeros_like(dsk)]).reshape(N_DIR, 1, dm.SD)
    est = 2 * CHUNK * dm.CD * 2 + 2 * CHUNK * dm.SD * 2 + SSM_STATE * dm.SD * 4 + (8 << 20)
    return pl.pallas_call(
        functools.partial(_ssd_kernel, SH=dm.SH, SG=dm.SG),
        grid=(dm.B, N_DIR, nl + ns),
        in_specs=[
            pl.BlockSpec((CHUNK, dm.CD), lambda b, d, s: (rb(b, d, s), 0)),
            pl.BlockSpec((CHUNK, LANES), lambda b, d, s: (rb(b, d, s), d)),
            pl.BlockSpec((None, 1, LANES), lambda b, d, s: (d, 0, 0)),
            pl.BlockSpec((None, 1, LANES), lambda b, d, s: (d, 0, 0)),
            pl.BlockSpec((None, 1, dm.SD), lambda b, d, s: (d, 0, 0)),
        ],
        out_specs=pl.BlockSpec((None, CHUNK, dm.SD), lambda b, d, s: (d, rb(b, d, s), 0)),
        out_shape=jax.ShapeDtypeStruct((N_DIR, dm.M, dm.SD), BF16),
        scratch_shapes=[pltpu.VMEM((SSM_STATE, dm.SD), F32)],
        compiler_params=_params(("arbitrary", "arbitrary", "arbitrary"), est),
        name="ssd_scan",
    )(u, dt_pad, bias_p, alog_p, dsk_p)


def _gate_kernel(yf_ref, yb_ref, z_ref, g_ref, o_ref):
    y = yf_ref[...].astype(F32) + yb_ref[...].astype(F32)
    o_ref[...] = _rms(y * _silu(z_ref[...].astype(F32)), g_ref[...]).astype(o_ref.dtype)


def _gate_call(dm, rows, y2, big, ssm_norm):
    tm = _pick(dm.L, (256, 128))
    sd = dm.SD
    zc = 0
    est = 2 * tm * sd * 2 * 4 + 6 * tm * sd * 4
    return pl.pallas_call(
        _gate_kernel,
        grid=(rows // tm,),
        in_specs=[
            pl.BlockSpec((None, tm, sd), lambda i: (0, i, 0)),
            pl.BlockSpec((None, tm, sd), lambda i: (1, i, 0)),
            pl.BlockSpec((tm, sd), lambda i: (i, zc)),
            pl.BlockSpec((1, sd), lambda i: (0, 0)),
        ],
        out_specs=pl.BlockSpec((tm, sd), lambda i: (i, 0)),
        out_shape=jax.ShapeDtypeStruct((rows, sd), BF16),
        compiler_params=_params(("arbitrary",), est),
        name="ssd_gate_norm",
    )(y2, y2, big, ssm_norm.reshape(1, sd))


def _attn_kernel(sink_ref, q_ref, kp_ref, kc_ref, kn_ref, vp_ref, vc_ref, vn_ref, kx_ref, vx_ref, g_ref,
                 o_ref, *, KVH, G, n_lat, L):
    i = pl.program_id(1)
    is_ctx = i >= n_lat
    band = 3 * BLOCK
    width = band + L
    rowq = lax.broadcasted_iota(jnp.int32, (BLOCK, width), 0)
    colk = lax.broadcasted_iota(jnp.int32, (BLOCK, width), 1)
    first_valid = jnp.where(i == 0, BLOCK, 0)
    last_valid = jnp.where(is_ctx, -1, jnp.where(i == n_lat - 1, 2 * BLOCK - 1, band - 1))
    lower = jnp.maximum(rowq, first_valid)
    upper = jnp.minimum(rowq + 2 * WINDOW, last_valid)
    neg_inf = jnp.float32(-jnp.inf)
    in_band = jnp.where(colk >= lower, jnp.where(colk <= upper, 0.0, neg_inf), neg_inf)
    bias = jnp.where(colk >= band, 0.0, in_band)
    scale = HEAD_DIM ** -0.5
    outs = []
    for hk in range(KVH):
        ks = slice(hk * HEAD_DIM, (hk + 1) * HEAD_DIM)
        qh = jnp.concatenate([q_ref[:, (hk * G + g) * HEAD_DIM:(hk * G + g + 1) * HEAD_DIM] for g in range(G)],
                             axis=0)
        kb = jnp.concatenate([kp_ref[:, ks], kc_ref[:, ks], kn_ref[:, ks], kx_ref[:, ks]], axis=0)
        vb = jnp.concatenate([vp_ref[:, ks], vc_ref[:, ks], vn_ref[:, ks], vx_ref[:, ks]], axis=0)
        sc = lax.dot_general(qh, kb, (((1,), (1,)), ((), ())), preferred_element_type=F32) * scale
        ps, dens = [], []
        for g in range(G):
            sg = sc[g * BLOCK:(g + 1) * BLOCK, :] + bias
            sk = sink_ref[hk * G + g]
            mx = jnp.maximum(jnp.max(sg, axis=-1, keepdims=True), sk)
            p = jnp.exp(sg - mx)
            dens.append(jnp.sum(p, axis=-1, keepdims=True) + jnp.exp(sk - mx))
            ps.append(p.astype(BF16))
        pv = jnp.dot(jnp.concatenate(ps, axis=0), vb, preferred_element_type=F32)
        for g in range(G):
            outs.append(pv[g * BLOCK:(g + 1) * BLOCK, :] / dens[g])
    o = jnp.concatenate(outs, axis=1)
    o_ref[...] = _rms(o, g_ref[...]).astype(o_ref.dtype)


def _attn_call(dm, qk, big, sink, attn_norm, with_ctx):
    ns, nl = dm.S // BLOCK, dm.L // BLOCK
    lat_blocks = dm.NL // BLOCK
    steps = ns + (nl if with_ctx else 0)
    rows = dm.M if with_ctx else dm.NL
    kd, ad = dm.KD, dm.AD
    kcol, vcol = ad // kd, (dm.SD + dm.CD) // kd

    def qrow(b, i):
        return jnp.where(i < ns, b * ns + i, lat_blocks + b * nl + (i - ns))

    def band_row(b, i, off):
        return b * ns + jnp.clip(i + off, 0, ns - 1)

    def band_spec(col, off):
        return pl.BlockSpec((BLOCK, kd), lambda b, i: (band_row(b, i, off), col))

    ctx_row0 = dm.NL // dm.L
    est = 2 * BLOCK * ad * 2 * 2 + 2 * 6 * BLOCK * kd * 2 + 2 * 2 * dm.L * kd * 2 + (12 << 20)
    return pl.pallas_call(
        functools.partial(_attn_kernel, KVH=dm.KVH, G=dm.AH // dm.KVH, n_lat=ns, L=dm.L),
        grid=(dm.B, steps),
        in_specs=[
            pl.BlockSpec(memory_space=pltpu.SMEM),
            pl.BlockSpec((BLOCK, ad), lambda b, i: (qrow(b, i), 0)),
            band_spec(kcol, -1), band_spec(kcol, 0), band_spec(kcol, 1),
            band_spec(vcol, -1), band_spec(vcol, 0), band_spec(vcol, 1),
            pl.BlockSpec((dm.L, kd), lambda b, i: (ctx_row0 + b, kcol)),
            pl.BlockSpec((dm.L, kd), lambda b, i: (ctx_row0 + b, vcol)),
            pl.BlockSpec((1, ad), lambda b, i: (0, 0)),
        ],
        out_specs=pl.BlockSpec((BLOCK, ad), lambda b, i: (qrow(b, i), 0)),
        out_shape=jax.ShapeDtypeStruct((rows, ad), BF16),
        compiler_params=_params(("arbitrary", "arbitrary"), est),
        name="band_attention",
    )(sink, qk, qk, qk, qk, big, big, big, qk, big, attn_norm.reshape(1, ad))


def _router_kernel(h_ref, r_ref, o_ref, *, E):
    x = h_ref[...]
    hi = x.astype(BF16)
    lo = (x - hi.astype(F32)).astype(BF16)
    r = r_ref[...]
    a = jnp.dot(hi, r, preferred_element_type=F32)
    b = jnp.dot(lo, r, preferred_element_type=F32)
    logits = (a[:, :LANES] + a[:, LANES:]) + (b[:, :LANES] + b[:, LANES:])
    lane = lax.broadcasted_iota(jnp.int32, logits.shape, 1)
    neg_inf = jnp.float32(-jnp.inf)
    lg = jnp.where(lane < E, logits, neg_inf)
    m1 = jnp.max(lg, axis=-1, keepdims=True)
    i1 = jnp.min(jnp.where(lg == m1, lane, LANES), axis=-1, keepdims=True)
    lg2 = jnp.where(lane == i1, neg_inf, lg)
    m2 = jnp.max(lg2, axis=-1, keepdims=True)
    i2 = jnp.min(jnp.where(lg2 == m2, lane, LANES), axis=-1, keepdims=True)
    e = jnp.exp(m2 - m1)
    g1 = 1.0 / (1.0 + e)
    g2 = e / (1.0 + e)
    o_ref[...] = jnp.where(lane == 0, i1.astype(F32),
                           jnp.where(lane == 1, i2.astype(F32),
                                     jnp.where(lane == 2, g1, jnp.where(lane == 3, g2, 0.0))))


def _router_call(dm, hf, router):
    tm = _pick(dm.NL, (512, 256, 128))
    r = jnp.pad(router, ((0, 0), (0, LANES - dm.E)))
    r_hi = r.astype(BF16)
    r_lo = (r - r_hi.astype(F32)).astype(BF16)
    r2 = jnp.concatenate([r_hi, r_lo], axis=1)
    est = 2 * tm * dm.D * 4 + 2 * dm.D * 2 * LANES * 2 + 4 * tm * dm.D * 2
    return pl.pallas_call(
        functools.partial(_router_kernel, E=dm.E),
        grid=(dm.NL // tm,),
        in_specs=[pl.BlockSpec((tm, dm.D), lambda i: (i, 0)),
                  pl.BlockSpec((dm.D, 2 * LANES), lambda i: (0, 0))],
        out_specs=pl.BlockSpec((tm, LANES), lambda i: (i, 0)),
        out_shape=jax.ShapeDtypeStruct((dm.NL, LANES), F32),
        compiler_params=_params(("arbitrary",), est),
        name="moe_router",
    )(hf, r2)


def _row_copy(src_hbm, row, dst, slot, sem):
    return pltpu.make_async_copy(src_hbm.at[pl.ds(row, 1), :], dst.at[pl.ds(slot, 1), :], sem)


ROW_COPY_UNROLL = 8


def _issue_rows(src_hbm, idx_ref, dst, sem, tm):
    def issue(r, carry):
        _row_copy(src_hbm, idx_ref[0, r], dst, r, sem).start()
        return carry

    lax.fori_loop(0, tm, issue, 0, unroll=ROW_COPY_UNROLL)


def _wait_rows(src_hbm, dst, sem, tm):
    pltpu.make_async_copy(src_hbm.at[pl.ds(0, tm), :], dst, sem).wait()


def _gather_kernel(nt_ref, src_ref, src_next_ref, h_hbm, o_ref, buf, sem, *, tm):
    t = pl.program_id(0)
    nt = nt_ref[0]
    slot = t % 2

    @pl.when(t == 0)
    def _():
        _issue_rows(h_hbm, src_ref, buf.at[0], sem.at[0], tm)

    @pl.when(t + 1 < nt)
    def _():
        _issue_rows(h_hbm, src_next_ref, buf.at[1 - slot], sem.at[1 - slot], tm)

    @pl.when(t < nt)
    def _():
        _wait_rows(h_hbm, buf.at[slot], sem.at[slot], tm)
        o_ref[...] = buf[slot].astype(o_ref.dtype)

    @pl.when(t >= nt)
    def _():
        o_ref[...] = jnp.zeros_like(o_ref)


def _gather_call(dm, hf, src_tok, n_tiles, tm, nt_max):
    src3 = src_tok.reshape(nt_max, 1, tm)
    grid_spec = pltpu.PrefetchScalarGridSpec(
        num_scalar_prefetch=1, grid=(nt_max,),
        in_specs=[pl.BlockSpec((None, 1, tm), lambda t, nt: (t, 0, 0), memory_space=pltpu.SMEM),
                  pl.BlockSpec((None, 1, tm), lambda t, nt: (jnp.minimum(t + 1, nt_max - 1), 0, 0),
                               memory_space=pltpu.SMEM),
                  pl.BlockSpec(memory_space=pl.ANY)],
        out_specs=pl.BlockSpec((tm, dm.D), lambda t, nt: (t, 0)),
        scratch_shapes=[pltpu.VMEM((2, tm, dm.D), F32), pltpu.SemaphoreType.DMA((2,))])
    est = 2 * tm * dm.D * 4 + 2 * tm * dm.D * 2 + 2 * tm * dm.D * 4
    return pl.pallas_call(
        functools.partial(_gather_kernel, tm=tm), grid_spec=grid_spec,
        out_shape=jax.ShapeDtypeStruct((nt_max * tm, dm.D), BF16),
        compiler_params=_params(("arbitrary",), est),
        name="moe_gather",
    )(n_tiles, src3, src3, hf)


def _combine_kernel(p1_ref, p2_ref, p1n_ref, p2n_ref, y_hbm, g1_ref, g2_ref, x_ref, gate_ref, gpost_ref, o_ref,
                    buf, sem, *, tm, n_blocks):
    t = pl.program_id(0)
    slot = t % 2

    def issue(first_ref, second_ref, s):
        _issue_rows(y_hbm, first_ref, buf.at[s, 0], sem.at[s], tm)
        _issue_rows(y_hbm, second_ref, buf.at[s, 1], sem.at[s], tm)

    @pl.when(t == 0)
    def _():
        issue(p1_ref, p2_ref, 0)

    @pl.when(t + 1 < n_blocks)
    def _():
        issue(p1n_ref, p2n_ref, 1 - slot)

    for k in range(2):
        _wait_rows(y_hbm, buf.at[slot, k], sem.at[slot], tm)
    f = g1_ref[...] * buf[slot, 0] + g2_ref[...] * buf[slot, 1]
    o_ref[...] = x_ref[...] + gate_ref[...] * _rms(f, gpost_ref[...])


def _combine_call(dm, y_sorted, pos1, pos2, g1, g2, x, mod3, gate_lk, g_post, tm):
    d = dm.D
    nblk = dm.NL // tm
    layer, k = gate_lk
    idx_spec = pl.BlockSpec((None, 1, tm), lambda i: (i, 0, 0), memory_space=pltpu.SMEM)
    idx_next = pl.BlockSpec((None, 1, tm), lambda i: (jnp.minimum(i + 1, nblk - 1), 0, 0),
                            memory_space=pltpu.SMEM)
    col_spec = pl.BlockSpec((tm, 1), lambda i: (i, 0))
    est = 4 * tm * d * 4 + 4 * tm * d * 4 + 4 * tm * d * 4 + 4 * tm * LANES * 4
    p1, p2 = pos1.reshape(nblk, 1, tm), pos2.reshape(nblk, 1, tm)
    return pl.pallas_call(
        functools.partial(_combine_kernel, tm=tm, n_blocks=nblk),
        grid=(nblk,),
        in_specs=[idx_spec, idx_spec, idx_next, idx_next, pl.BlockSpec(memory_space=pl.ANY), col_spec, col_spec,
                  pl.BlockSpec((tm, d), lambda i: (i, 0)),
                  pl.BlockSpec((None, 1, d),
                               lambda i: ((layer * MOD_ROWS + (i * tm) // dm.S) * N_MOD + k, 0, 0)),
                  pl.BlockSpec((1, d), lambda i: (0, 0))],
        out_specs=pl.BlockSpec((tm, d), lambda i: (i, 0)),
        out_shape=jax.ShapeDtypeStruct((dm.NL, d), F32),
        scratch_shapes=[pltpu.VMEM((2, 2, tm, d), F32), pltpu.SemaphoreType.DMA((2,))],
        compiler_params=_params(("arbitrary",), est),
        name="moe_combine_residual",
    )(p1, p2, p1, p2, y_sorted, g1.reshape(dm.NL, 1), g2.reshape(dm.NL, 1), x, mod3, g_post.reshape(1, d))


def _moe_plan(dm, rout, tm):
    t = dm.NL
    e_flat = jnp.concatenate([rout[:, 0], rout[:, 1]]).astype(jnp.int32)
    onehot = (e_flat[:, None] == jnp.arange(dm.E, dtype=jnp.int32)[None, :]).astype(jnp.int32)
    before = jnp.cumsum(onehot, axis=0) - onehot
    rank = jnp.sum(before * onehot, axis=1)
    counts = jnp.sum(onehot, axis=0)
    tiles = (counts + tm - 1) // tm
    tile_end = jnp.cumsum(tiles)
    tile_start = tile_end - tiles
    dest = jnp.sum(onehot * tile_start[None, :], axis=1) * tm + rank
    nt_max = (2 * t) // tm + dm.E
    n_tiles = tile_end[-1:].astype(jnp.int32)
    tile_ids = jnp.arange(nt_max, dtype=jnp.int32)
    tile_expert = jnp.minimum(jnp.sum((tile_ids[:, None] >= tile_end[None, :]).astype(jnp.int32), axis=1),
                              dm.E - 1).astype(jnp.int32)
    ids = jnp.arange(dm.E, dtype=jnp.int32)
    later = jnp.logical_and(tiles[None, :] > 0, ids[None, :] > ids[:, None])
    next_of = jnp.min(jnp.where(later, ids[None, :], dm.E), axis=1)
    next_of = jnp.where(next_of == dm.E, -1, next_of).astype(jnp.int32)
    next_expert = jnp.sum(jnp.where(tile_expert[:, None] == ids[None, :], next_of[None, :], 0), axis=1)
    tok = jnp.concatenate([jnp.arange(t, dtype=jnp.int32)] * 2)
    src_tok = jnp.zeros((nt_max * tm,), jnp.int32).at[dest].set(tok)
    return src_tok, dest[:t], dest[t:], (tile_expert, next_expert.astype(jnp.int32), n_tiles), nt_max


def _rope_tables(dm):
    half = HEAD_DIM // 2
    freqs = ROPE_THETA ** (-jnp.arange(0, half, 2, dtype=F32) / half)
    pos = jnp.arange(dm.S, dtype=jnp.int32)
    row = (pos // dm.GW).astype(F32)[:, None] * freqs
    col = (pos % dm.GW).astype(F32)[:, None] * freqs
    cos = jnp.concatenate([jnp.cos(row), jnp.cos(row), jnp.cos(col), jnp.cos(col)], axis=1)
    sin = jnp.concatenate([-jnp.sin(row), jnp.sin(row), -jnp.sin(col), jnp.sin(col)], axis=1)
    cos = jnp.concatenate([jnp.tile(cos, (dm.B, 1)), jnp.ones((dm.NC, HEAD_DIM), F32)], axis=0)
    sin = jnp.concatenate([jnp.tile(sin, (dm.B, 1)), jnp.zeros((dm.NC, HEAD_DIM), F32)], axis=0)
    return cos, sin


def _trunk(dm, x, c, ctx, c_ctx, ada_w, ada_b, norm_mix_pre, norm_mix_post, norm_ffn_pre, norm_ffn_post,
           w_in, attn_sink, attn_norm, conv_w, conv_b, dt_bias, a_log, d_skip, ssm_norm, w_out,
           ffn_w_gate, ffn_w_up, ffn_w_down, moe_router, moe_w_gate, moe_w_up, moe_w_down):
    depth = ada_w.shape[0]
    d = dm.D
    x2 = x.reshape(dm.NL, d)
    c2 = ctx.reshape(dm.NC, d)
    cvec = jnp.concatenate([c, c_ctx[None, :], jnp.zeros((MOD_ROWS - dm.B - 1, d), F32)], axis=0)
    mod3 = _mod_call(cvec, ada_w, ada_b).reshape(depth * MOD_ROWS * N_MOD, 1, d)
    cos, sin = _rope_tables(dm)

    tm_all = _row_tile(dm.M, MM_ROW_TILE_CAP)
    tm_lat = _row_tile(dm.NL, MM_ROW_TILE_CAP)
    tn_in = _pick(math.gcd(math.gcd(dm.AD, dm.SD), math.gcd(dm.KD, dm.CD)), (512, 256, 128))
    nq, nz, nk, nx = dm.AD // tn_in, dm.SD // tn_in, dm.KD // tn_in, dm.CD // tn_in
    src_qk = lambda j: jnp.where(j < nq, j, j + nz)
    src_zxv = lambda j: jnp.where(j < nz, j + nq,
                                  jnp.where(j < nz + nx, j + nq + 2 * nk, j - nx + nq + nk))
    w_in_t = jnp.swapaxes(w_in, 1, 2)
    w_dt = w_in_t[:, dm.DT0:, :].reshape(depth, N_DIR, dm.SH, d)
    w_dt = jnp.pad(w_dt, ((0, 0), (0, 0), (0, LANES - dm.SH), (0, 0))).reshape(depth, N_DIR * LANES, d)

    xa = None
    for i in range(depth):
        last = i == depth - 1
        if i == 0:
            _, h = _norm_call(dm, dm.M, x2, c2, None, None, None, norm_mix_pre[i], (i, 0), (i, 1), mod3, BF16)
        qk = _mm_call([h], [w_in_t], i, rows=dm.M, tm=tm_all, tn=tn_in, n_cols=dm.AD + dm.KD, out_dtype=BF16,
                      rope=(cos, sin), w_nt=True, panel_src=src_qk, name="proj_in_qk")
        big = _mm_call([h], [w_in_t], i, rows=dm.M, tm=tm_all, tn=tn_in, n_cols=dm.SD + dm.CD + dm.KD,
                       out_dtype=BF16, w_nt=True, panel_src=src_zxv, name="proj_in_zxv")
        dt_pad = _mm_call([h], [w_dt], i, rows=dm.M, tm=tm_all, tn=N_DIR * LANES, n_cols=N_DIR * LANES,
                          out_dtype=F32, w_nt=True, name="proj_dt")
        u = _conv_call(dm, big, conv_w[i], conv_b[i])
        y2 = _ssd_call(dm, u, dt_pad, dt_bias[i], a_log[i], d_skip[i])
        rows = dm.NL if last else dm.M
        tm_rows = tm_lat if last else tm_all
        ssm = _gate_call(dm, rows, y2, big, ssm_norm[i])
        attn = _attn_call(dm, qk, big, attn_sink[i], attn_norm[i], with_ctx=not last)
        y = _mm_call([attn, ssm], [w_out], i, rows=rows, tm=tm_rows, tn=_pick(d, (512, 256, 128)), n_cols=d,
                     out_dtype=BF16, name="proj_out")
        if i == 0:
            xa, hf = _norm_call(dm, rows, x2, c2, y, norm_mix_post[i], (i, 2), norm_ffn_pre[i], (i, 3), (i, 4),
                                mod3, BF16 if i % 2 == 0 else F32)
        else:
            xa, hf = _norm_call(dm, rows, xa, None, y, norm_mix_post[i], (i, 2), norm_ffn_pre[i], (i, 3), (i, 4),
                                mod3, BF16 if i % 2 == 0 else F32)
        j = i // 2
        if i % 2 == 0:
            act = _mm_call([hf], [ffn_w_gate, ffn_w_up], j, rows=rows, tm=tm_rows,
                           tn=_pick(dm.DFF, (256, 128)), n_cols=dm.DFF, out_dtype=BF16, mode="swiglu",
                           name="ffn_gate_up")
            f = _mm_call([act], [ffn_w_down], j, rows=rows, tm=_row_tile(rows, MM_ROW_TILE_CAP // 4),
                         tn=_pick(d, (512, 256, 128)), n_cols=d, out_dtype=BF16, name="ffn_down")
            if last:
                raise NotImplementedError("dense FFN on the last layer")
            xa, h = _norm_call(dm, rows, xa, None, f, norm_ffn_post[i], (i, 5), norm_mix_pre[i + 1],
                               (i + 1, 0), (i + 1, 1), mod3, BF16)
        else:
            if not last:
                raise NotImplementedError("MoE FFN on a non-final layer")
            tm_e = _pick(dm.NL, (256, 128))
            rout = _router_call(dm, hf, moe_router[j])
            src_tok, pos1, pos2, tables, nt_max = _moe_plan(dm, rout, tm_e)
            xs = _gather_call(dm, hf, src_tok, tables[2], tm_e, nt_max)
            n_exp = moe_w_gate.shape[1]
            wg = moe_w_gate.reshape((-1,) + moe_w_gate.shape[2:])
            wu = moe_w_up.reshape((-1,) + moe_w_up.shape[2:])
            wd = moe_w_down.reshape((-1,) + moe_w_down.shape[2:])
            act = _mm_call([xs], [wg, wu], j * n_exp, rows=nt_max * tm_e, tm=tm_e,
                           tn=_pick(dm.DE, (512, 256, 128)), n_cols=dm.DE, out_dtype=BF16, mode="swiglu",
                           grouped=tables, name="moe_gate_up")
            ys = _mm_call([act], [wd], j * n_exp, rows=nt_max * tm_e, tm=tm_e,
                          tn=_pick(d, (1024, 512, 256, 128)), n_cols=d, out_dtype=F32,
                          grouped=tables, name="moe_down")
            xa = _combine_call(dm, ys, pos1, pos2, rout[:, 2], rout[:, 3], xa, mod3, (i, 5), norm_ffn_post[i],
                               tm_e)
    return xa[:dm.NL].reshape(dm.B, dm.S, d)


def kernel(x, c, ctx, c_ctx, ada_w, ada_b, norm_mix_pre, norm_mix_post, norm_ffn_pre, norm_ffn_post, w_in,
           attn_sink, attn_norm, conv_w, conv_b, dt_bias, a_log, d_skip, ssm_norm, w_out, ffn_w_gate, ffn_w_up,
           ffn_w_down, moe_router, moe_w_gate, moe_w_up, moe_w_down):
    b, s, d = x.shape
    sh = a_log.shape[2]
    cd = conv_w.shape[2]
    sd = sh * SSM_HEAD_DIM
    dm = Dims(B=b, S=s, L=ctx.shape[1], D=d, AH=attn_sink.shape[1], KVH=ATTN_KV_HEADS, SH=sh,
              SG=(cd - sd) // (2 * SSM_STATE), DFF=ffn_w_gate.shape[2], E=moe_router.shape[2],
              DE=moe_w_gate.shape[3], GW=GRID_W)
    assert w_in.shape[2] == dm.DT0 + N_DIR * dm.SH and w_out.shape[1] == dm.AD + dm.SD and dm.AD == dm.SD
    assert dm.S % BLOCK == 0 and dm.L % BLOCK == 0 and dm.SH <= LANES and dm.E <= LANES
    return _trunk(dm, x, c, ctx, c_ctx, ada_w, ada_b, norm_mix_pre, norm_mix_post, norm_ffn_pre, norm_ffn_post,
                  w_in, attn_sink, attn_norm, conv_w, conv_b, dt_bias, a_log, d_skip, ssm_norm, w_out,
                  ffn_w_gate, ffn_w_up, ffn_w_down, moe_router, moe_w_gate, moe_w_up, moe_w_down)
```

```python
import functools
import math
from typing import NamedTuple

import jax
import jax.numpy as jnp
from jax import lax
from jax.experimental import pallas as pl
from jax.experimental.pallas import tpu as pltpu

F32 = jnp.float32
BF16 = jnp.bfloat16

HEAD_DIM = 128
WINDOW = 128
BLOCK = 128
ROPE_THETA = 10000.0
SSM_HEAD_DIM = 64
SSM_STATE = 128
CONV_K = 5
CHUNK = 128
N_DIR = 2
GRID_W = 64
ATTN_KV_HEADS = 4
N_MOD = 6
EPS = 1e-6
MOD_ROWS = 8

V7X_VMEM_BYTES = 64 * 1024 * 1024
VMEM_LIMIT_CAP = 56 * 1024 * 1024
LANES = 128
BF16_SUBLANES = 16
MM_ROW_TILE_CAP = 1100
MM_ROW_CHUNKS = 4
MM_MIN_CHUNK_ROWS = 256


class Dims(NamedTuple):
    B: int
    S: int
    L: int
    D: int
    AH: int
    KVH: int
    SH: int
    SG: int
    DFF: int
    E: int
    DE: int
    GW: int

    @property
    def AD(self):
        return self.AH * HEAD_DIM

    @property
    def KD(self):
        return self.KVH * HEAD_DIM

    @property
    def SD(self):
        return self.SH * SSM_HEAD_DIM

    @property
    def BC(self):
        return self.SG * SSM_STATE

    @property
    def CD(self):
        return self.SD + 2 * self.BC

    @property
    def K0(self):
        return self.AD + self.SD

    @property
    def V0(self):
        return self.K0 + self.KD

    @property
    def X0(self):
        return self.V0 + self.KD

    @property
    def DT0(self):
        return self.X0 + self.CD

    @property
    def NL(self):
        return self.B * self.S

    @property
    def NC(self):
        return self.B * self.L

    @property
    def M(self):
        return self.NL + self.NC


def _pick(n, prefs):
    for p in prefs:
        if n % p == 0:
            return p
    raise ValueError(f"no tile in {prefs} divides {n}")


def _row_tile(rows, cap):
    for t in range(min(cap, rows), 0, -1):
        if rows % t == 0 and t % BF16_SUBLANES == 0:
            return t
    raise ValueError(f"no row tile <= {cap} divides {rows}")


def _row_chunks(tm, want):
    for c in range(want, 0, -1):
        if tm % c == 0 and (tm // c) % BF16_SUBLANES == 0 and tm // c >= MM_MIN_CHUNK_ROWS:
            return c
    return 1


def _params(sem, est_bytes):
    limit = int(min(max(est_bytes * 5 // 4 + (2 << 20), 16 << 20), VMEM_LIMIT_CAP))
    return pltpu.CompilerParams(dimension_semantics=sem, vmem_limit_bytes=limit)


def _silu(v):
    return v / (1.0 + jnp.exp(-v))


def _rms(v, g):
    return v * lax.rsqrt(jnp.mean(v * v, axis=-1, keepdims=True) + EPS) * g


def _mod_kernel(c_ref, w_ref, b_ref, o_ref):
    c = _silu(c_ref[...]).astype(BF16)
    o_ref[...] = jnp.dot(c, w_ref[...].astype(BF16), preferred_element_type=F32) + b_ref[...]


def _mod_call(cvec, ada_w, ada_b):
    depth, d, n = ada_w.shape
    tn = _pick(n, (512, 256, 128))
    est = 2 * d * tn * 4 + d * tn * 2 + 4 * MOD_ROWS * tn * 4
    return pl.pallas_call(
        _mod_kernel,
        grid=(depth, n // tn),
        in_specs=[
            pl.BlockSpec((MOD_ROWS, d), lambda l, j: (0, 0)),
            pl.BlockSpec((None, d, tn), lambda l, j: (l, 0, j)),
            pl.BlockSpec((None, 1, tn), lambda l, j: (l, 0, j)),
        ],
        out_specs=pl.BlockSpec((None, MOD_ROWS, tn), lambda l, j: (l, 0, j)),
        out_shape=jax.ShapeDtypeStruct((depth, MOD_ROWS, n), F32),
        compiler_params=_params(("arbitrary", "arbitrary"), est),
        name="adaln_mod",
    )(cvec, ada_w, ada_b.reshape(depth, 1, n))


def _norm_kernel(*refs, split, has_res, n_lat_blocks):
    refs = list(refs)
    x_ref = refs.pop(0)
    c_ref = refs.pop(0) if split else None
    if has_res:
        y_ref, gpost_ref, gate_ref = refs.pop(0), refs.pop(0), refs.pop(0)
    gpre_ref, sh_ref, sc_ref = refs.pop(0), refs.pop(0), refs.pop(0)
    if has_res:
        xo_ref = refs.pop(0)
    h_ref = refs.pop(0)

    def run(src_ref):
        x = src_ref[...]
        if has_res:
            x = x + gate_ref[...] * _rms(y_ref[...].astype(F32), gpost_ref[...])
            xo_ref[...] = x
        h_ref[...] = (_rms(x, gpre_ref[...]) * (1.0 + sc_ref[...]) + sh_ref[...]).astype(h_ref.dtype)

    if split:
        i = pl.program_id(0)
        pl.when(i < n_lat_blocks)(lambda: run(x_ref))
        pl.when(i >= n_lat_blocks)(lambda: run(c_ref))
    else:
        run(x_ref)


def _norm_call(dm, rows, x, ctx2d, y, g_post, gate_lk, g_pre, sh_lk, sc_lk, mod3, h_dtype):
    d = dm.D
    tm = _pick(dm.L, (256, 128))
    nblk = rows // tm
    nlat = dm.NL // tm
    split = ctx2d is not None
    has_res = y is not None

    def mod_spec(lk):
        layer, k = lk
        return pl.BlockSpec(
            (None, 1, d),
            lambda i: ((layer * MOD_ROWS + jnp.minimum((i * tm) // dm.S, dm.B)) * N_MOD + k, 0, 0))

    row_spec = pl.BlockSpec((tm, d), lambda i: (i, 0))
    vec_spec = pl.BlockSpec((1, d), lambda i: (0, 0))
    in_specs, args = [], []
    if split:
        in_specs += [pl.BlockSpec((tm, d), lambda i: (jnp.minimum(i, nlat - 1), 0)),
                     pl.BlockSpec((tm, d), lambda i: (jnp.maximum(i - nlat, 0), 0))]
        args += [x, ctx2d]
    else:
        in_specs += [row_spec]
        args += [x]
    if has_res:
        in_specs += [row_spec, vec_spec, mod_spec(gate_lk)]
        args += [y, g_post.reshape(1, d), mod3]
    in_specs += [vec_spec, mod_spec(sh_lk), mod_spec(sc_lk)]
    args += [g_pre.reshape(1, d), mod3, mod3]
    out_specs, out_shape = [], []
    if has_res:
        out_specs.append(row_spec)
        out_shape.append(jax.ShapeDtypeStruct((rows, d), F32))
    out_specs.append(row_spec)
    out_shape.append(jax.ShapeDtypeStruct((rows, d), h_dtype))
    est = 2 * tm * d * (4 + 4 + 2 + 4 + 4)
    outs = pl.pallas_call(
        functools.partial(_norm_kernel, split=split, has_res=has_res, n_lat_blocks=nlat),
        grid=(nblk,),
        in_specs=in_specs,
        out_specs=out_specs,
        out_shape=out_shape,
        compiler_params=_params(("arbitrary",), est),
        name="residual_prenorm" if has_res else "prenorm",
    )(*args)
    return outs if has_res else (None, outs[0])


def _rope(acc, cos, sin_signed, tn):
    lane = lax.broadcasted_iota(jnp.int32, acc.shape, 1)
    first = (lane % 64) < 32
    partner = jnp.where(first, pltpu.roll(acc, tn - 32, 1), pltpu.roll(acc, 32, 1))
    reps = tn // HEAD_DIM
    cos_t = jnp.concatenate([cos] * reps, axis=1) if reps > 1 else cos
    sin_t = jnp.concatenate([sin_signed] * reps, axis=1) if reps > 1 else sin_signed
    return acc * cos_t + partner * sin_t


def _mm_kernel(*refs, n_lhs, k_sizes, n_w, mode, with_rope, grouped, tn, w_nt, w_group, n_panels, panel_src,
               row_chunks):
    refs = list(refs)
    if grouped:
        te_ref, nxt_ref, nt_ref = refs.pop(0), refs.pop(0), refs.pop(0)
    lhs_refs = [refs.pop(0) for _ in range(n_lhs)]
    w_refs = [refs.pop(0) for _ in range(n_w)]
    if with_rope:
        cos_ref, sin_ref = refs.pop(0), refs.pop(0)
    o_ref = refs.pop(0)
    wb_refs = [refs.pop(0) for _ in range(n_w)]
    stage_refs = [refs.pop(0) for _ in range(n_w)]
    sem = refs.pop(0)
    j = pl.program_id(0)
    m = pl.program_id(1)

    def panel_copy(i, g, jp):
        col = pl.multiple_of(panel_src(jp) * tn, tn)
        src = w_refs[i].at[g, pl.ds(col, tn), :] if w_nt else w_refs[i].at[g, :, pl.ds(col, tn)]
        return pltpu.make_async_copy(src, stage_refs[i], sem.at[i])

    if grouped:
        nt = nt_ref[0]
        active = m < nt
        new_panel = jnp.logical_or(m == 0, jnp.logical_and(active, te_ref[m] != te_ref[jnp.maximum(m - 1, 0)]))
        g_cur = w_group + te_ref[m]
        more_here = nxt_ref[m] >= 0
        g_next = w_group + jnp.where(more_here, nxt_ref[m], te_ref[0])
        j_next = jnp.where(more_here, j, j + 1)
        has_next = jnp.logical_or(more_here, j + 1 < n_panels)
    else:
        active = None
        new_panel = m == 0
        g_cur = g_next = w_group
        j_next = j + 1
        has_next = j + 1 < n_panels

    @pl.when(jnp.logical_and(j == 0, m == 0))
    def _():
        for i in range(n_w):
            panel_copy(i, g_cur, j).start()

    @pl.when(new_panel)
    def _():
        for i in range(n_w):
            panel_copy(i, g_cur, j).wait()
            wb_refs[i][...] = stage_refs[i][...].astype(BF16)

        @pl.when(has_next)
        def _():
            for i in range(n_w):
                panel_copy(i, g_next, j_next).start(priority=1)

    def compute():
        rc = o_ref.shape[0] // row_chunks
        for c in range(row_chunks):
            rows = slice(c * rc, (c + 1) * rc)
            accs = []
            for wb_ref in wb_refs:
                acc, k0 = None, 0
                for l_ref, ks in zip(lhs_refs, k_sizes):
                    if w_nt:
                        part = lax.dot_general(l_ref[rows, :], wb_ref[:, k0:k0 + ks], (((1,), (1,)), ((), ())),
                                               preferred_element_type=F32)
                    else:
                        part = jnp.dot(l_ref[rows, :], wb_ref[k0:k0 + ks, :], preferred_element_type=F32)
                    acc = part if acc is None else acc + part
                    k0 += ks
                accs.append(acc)
            res = _silu(accs[0]) * accs[1] if mode == "swiglu" else accs[0]
            if with_rope:
                res = _rope(res, cos_ref[rows, :], sin_ref[rows, :], tn)
            o_ref[rows, :] = res.astype(o_ref.dtype)

    if grouped:
        pl.when(active)(compute)

        @pl.when(jnp.logical_not(active))
        def _():
            o_ref[...] = jnp.zeros_like(o_ref)
    else:
        compute()


def _mm_call(lhs_list, w_list, w_group, *, rows, tm, tn, n_cols, out_dtype, mode="plain",
             rope=None, grouped=None, w_nt=False, panel_src=None, name="panel_matmul"):
    k_sizes = tuple(a.shape[1] for a in lhs_list)
    k_tot = sum(k_sizes)
    n_w = len(w_list)
    assert all(w.shape[2 if w_nt else 1] == k_tot for w in w_list)
    n_panels = n_cols // tn
    n_blocks = rows // tm
    args, in_specs = [], []
    src = panel_src if panel_src is not None else (lambda j: j)
    if grouped is not None:
        lhs_map = lambda j, m, te, nx, nt: (jnp.minimum(m, nt[0] - 1), 0)
        out_map = lambda j, m, te, nx, nt: (m, j)
    else:
        lhs_map = lambda j, m: (m, 0)
        out_map = lambda j, m: (m, j)
    for a, ks in zip(lhs_list, k_sizes):
        in_specs.append(pl.BlockSpec((tm, ks), lhs_map))
        args.append(a)
    w_block = (tn, k_tot) if w_nt else (k_tot, tn)
    for w in w_list:
        in_specs.append(pl.BlockSpec(memory_space=pl.ANY))
        args.append(w)
    if rope is not None:
        cos, sin_signed = rope
        in_specs += [pl.BlockSpec((tm, HEAD_DIM), lambda j, m: (m, 0))] * 2
        args += [cos, sin_signed]
    out_bytes = jnp.dtype(out_dtype).itemsize
    est = (n_w * k_tot * tn * (4 + 2) + 2 * tm * k_tot * 2 + 2 * tm * tn * out_bytes + (2 + n_w) * tm * tn * 4)
    kern = functools.partial(_mm_kernel, n_lhs=len(lhs_list), k_sizes=k_sizes, n_w=n_w, mode=mode,
                             with_rope=rope is not None, grouped=grouped is not None, tn=tn, w_nt=w_nt,
                             w_group=w_group, n_panels=n_panels, panel_src=src,
                             row_chunks=_row_chunks(tm, MM_ROW_CHUNKS))
    scratch = ([pltpu.VMEM(w_block, BF16) for _ in range(n_w)] + [pltpu.VMEM(w_block, F32) for _ in range(n_w)]
               + [pltpu.SemaphoreType.DMA((n_w,))])
    out_shape = jax.ShapeDtypeStruct((rows, n_cols), out_dtype)
    cp = _params(("arbitrary", "arbitrary"), est)
    if grouped is not None:
        grid_spec = pltpu.PrefetchScalarGridSpec(
            num_scalar_prefetch=3, grid=(n_panels, n_blocks), in_specs=in_specs,
            out_specs=pl.BlockSpec((tm, tn), out_map), scratch_shapes=scratch)
        return pl.pallas_call(kern, grid_spec=grid_spec, out_shape=out_shape, compiler_params=cp,
                              name=name)(*grouped, *args)
    return pl.pallas_call(kern, grid=(n_panels, n_blocks), in_specs=in_specs,
                          out_specs=pl.BlockSpec((tm, tn), out_map), out_shape=out_shape,
                          scratch_shapes=scratch, compiler_params=cp, name=name)(*args)


def _conv_kernel(prev_ref, cur_ref, next_ref, w_ref, b_ref, o_ref, *, tb, seg_lat, seg_ctx, n_lat_rows):
    i = pl.program_id(0)
    row0 = i * tb
    in_lat = row0 < n_lat_rows
    off = jnp.where(in_lat, row0 % seg_lat, (row0 - n_lat_rows) % seg_ctx)
    seg = jnp.where(in_lat, seg_lat, seg_ctx)
    keep_prev = jnp.where(off != 0, 1.0, 0.0)
    keep_next = jnp.where(off + tb != seg, 1.0, 0.0)
    halo = BF16_SUBLANES
    prev = prev_ref[...].astype(F32)[halo - 8:, :] * keep_prev
    nxt = next_ref[...].astype(F32)[:8, :] * keep_next
    ext = jnp.concatenate([prev, cur_ref[...].astype(F32), nxt], axis=0)
    rows = tb + 16
    pad = CONV_K // 2
    acc = None
    for k in range(CONV_K):
        shift = (pad - k) % rows
        shifted = ext if shift == 0 else pltpu.roll(ext, shift, 0)
        term = shifted[8:8 + tb, :] * w_ref[k:k + 1, :]
        acc = term if acc is None else acc + term
    o_ref[...] = _silu(acc + b_ref[...]).astype(o_ref.dtype)


def _conv_call(dm, big, conv_w, conv_b):
    tb = _pick(dm.L, (256, 128))
    cw = _pick(math.gcd(dm.SD, dm.CD), (1024, 512, 256, 128))
    halo = BF16_SUBLANES
    col0 = dm.SD // cw
    nrb = dm.M // tb
    last_halo = dm.M // halo - 1
    est = 2 * (tb + 2 * halo) * cw * 2 + 2 * tb * cw * 2 + 8 * (tb + 16) * cw * 4
    return pl.pallas_call(
        functools.partial(_conv_kernel, tb=tb, seg_lat=dm.S, seg_ctx=dm.L, n_lat_rows=dm.NL),
        grid=(nrb, dm.CD // cw),
        in_specs=[
            pl.BlockSpec((halo, cw), lambda i, j: (jnp.maximum(i * (tb // halo) - 1, 0), col0 + j)),
            pl.BlockSpec((tb, cw), lambda i, j: (i, col0 + j)),
            pl.BlockSpec((halo, cw), lambda i, j: (jnp.minimum((i + 1) * (tb // halo), last_halo), col0 + j)),
            pl.BlockSpec((8, cw), lambda i, j: (0, j)),
            pl.BlockSpec((1, cw), lambda i, j: (0, j)),
        ],
        out_specs=pl.BlockSpec((tb, cw), lambda i, j: (i, j)),
        out_shape=jax.ShapeDtypeStruct((dm.M, dm.CD), BF16),
        compiler_params=_params(("arbitrary", "arbitrary"), est),
        name="dwconv_silu",
    )(big, big, big, jnp.pad(conv_w, ((0, 8 - CONV_K), (0, 0))), conv_b.reshape(1, dm.CD))


def _split3(v):
    hi = v.astype(BF16)
    r1 = v - hi.astype(F32)
    mid = r1.astype(BF16)
    lo = (r1 - mid.astype(F32)).astype(BF16)
    return hi, mid, lo


def _softplus(v):
    t = jnp.exp(-jnp.abs(v))
    y = 1.0 + t
    return jnp.maximum(v, 0.0) + (jnp.log(y) - ((y - 1.0) - t) / y)


def _ssd_kernel(u_ref, dt_ref, bias_ref, alog_ref, dsk_ref, y_ref, st_ref, *, SH, SG):
    d = pl.program_id(1)
    s = pl.program_id(2)
    hpg = SH // SG
    sd = SH * SSM_HEAD_DIM
    bc = SG * SSM_STATE
    c = CHUNK

    @pl.when(s == 0)
    def _():
        st_ref[...] = jnp.zeros_like(st_ref)

    dt = _softplus(dt_ref[...] + bias_ref[...])
    a = dt * (-jnp.exp(alog_ref[...]))
    row = lax.broadcasted_iota(jnp.int32, (c, c), 0)
    col = lax.broadcasted_iota(jnp.int32, (c, c), 1)
    fwd = d == 0
    tri = jnp.where(fwd, col, row) <= jnp.where(fwd, row, col)
    hi, mid, lo = _split3(a)
    tri_b = jnp.where(tri, 1.0, 0.0).astype(BF16)
    parts = jnp.dot(tri_b, jnp.concatenate([hi, mid, lo], axis=1), preferred_element_type=F32)
    cum = parts[:, :LANES] + parts[:, LANES:2 * LANES] + parts[:, 2 * LANES:]
    tot = jnp.sum(a, axis=0, keepdims=True)
    w_t = (jnp.exp(tot - cum) * dt).T
    cum_t = cum.T
    dt_t = dt.T
    etot = jnp.exp(tot)
    lane_lo = lax.broadcasted_iota(jnp.int32, (c, LANES), 1) < SSM_HEAD_DIM
    lane_lo1 = lax.broadcasted_iota(jnp.int32, (1, LANES), 1) < SSM_HEAD_DIM
    neg_inf = jnp.float32(-jnp.inf)

    for g in range(SG):
        b_g = u_ref[:, sd + g * SSM_STATE: sd + (g + 1) * SSM_STATE]
        c_g = u_ref[:, sd + bc + g * SSM_STATE: sd + bc + (g + 1) * SSM_STATE]
        cb = lax.dot_general(c_g, b_g, (((1,), (1,)), ((), ())), preferred_element_type=F32)
        bt_g = b_g.astype(F32).T
        c_g32 = c_g.astype(F32)
        for pr in range(hpg // 2):
            h0 = g * hpg + 2 * pr
            lo_col, hi_col = h0 * SSM_HEAD_DIM, (h0 + 2) * SSM_HEAD_DIM
            xs_pair = u_ref[:, lo_col:hi_col]
            st_pair = st_ref[:, lo_col:hi_col]
            rhs = jnp.concatenate([xs_pair, st_pair.astype(BF16)], axis=0)
            ys, sts, ets = [], [], []
            for q in range(2):
                h = h0 + q
                colb = jnp.broadcast_to(cum[:, h:h + 1], (c, c))
                decay = jnp.exp(jnp.where(tri, colb - cum_t[h:h + 1, :], neg_inf))
                m_h = decay * cb * dt_t[h:h + 1, :]
                c_h = c_g32 * jnp.exp(colb)
                lhs = jnp.concatenate([m_h, c_h], axis=1).astype(BF16)
                ys.append(jnp.dot(lhs, rhs, preferred_element_type=F32))
                btw = (bt_g * w_t[h:h + 1, :]).astype(BF16)
                sts.append(jnp.dot(btw, xs_pair, preferred_element_type=F32))
                ets.append(jnp.broadcast_to(etot[:, h:h + 1], (1, LANES)))
            y_pair = jnp.where(lane_lo, ys[0], ys[1]) + dsk_ref[:, lo_col:hi_col] * xs_pair.astype(F32)
            e_pair = jnp.where(lane_lo1, ets[0], ets[1])
            st_ref[:, lo_col:hi_col] = e_pair * st_pair + jnp.where(lane_lo, sts[0], sts[1])
            y_ref[:, lo_col:hi_col] = y_pair.astype(y_ref.dtype)


def _ssd_call(dm, u, dt_pad, dt_bias, a_log, d_skip):
    nl, ns = dm.L // CHUNK, dm.S // CHUNK
    lat_blocks = dm.NL // CHUNK

    def rb(b, d, s):
        ctx_chunk = jnp.where(d == 0, s, nl - 1 - s)
        lat_chunk = jnp.where(d == 0, s - nl, ns - 1 - (s - nl))
        return jnp.where(s < nl, lat_blocks + b * nl + ctx_chunk, b * ns + lat_chunk)

    pad = LANES - dm.SH
    bias_p = jnp.pad(dt_bias, ((0, 0), (0, pad))).reshape(N_DIR, 1, LANES)
    alog_p = jnp.pad(a_log, ((0, 0), (0, pad))).reshape(N_DIR, 1, LANES)
    dsk = jnp.repeat(d_skip, SSM_HEAD_DIM)
    dsk_p = jnp.stack([dsk, jnp.zeros_like(dsk)]).reshape(N_DIR, 1, dm.SD)
    est = 2 * CHUNK * dm.CD * 2 + 2 * CHUNK * dm.SD * 2 + SSM_STATE * dm.SD * 4 + (8 << 20)
    return pl.pallas_call(
        functools.partial(_ssd_kernel, SH=dm.SH, SG=dm.SG),
        grid=(dm.B, N_DIR, nl + ns),
        in_specs=[
            pl.BlockSpec((CHUNK, dm.CD), lambda b, d, s: (rb(b, d, s), 0)),
            pl.BlockSpec((CHUNK, LANES), lambda b, d, s: (rb(b, d, s), d)),
            pl.BlockSpec((None, 1, LANES), lambda b, d, s: (d, 0, 0)),
            pl.BlockSpec((None, 1, LANES), lambda b, d, s: (d, 0, 0)),
            pl.BlockSpec((None, 1, dm.SD), lambda b, d, s: (d, 0, 0)),
        ],
        out_specs=pl.BlockSpec((None, CHUNK, dm.SD), lambda b, d, s: (d, rb(b, d, s), 0)),
        out_shape=jax.ShapeDtypeStruct((N_DIR, dm.M, dm.SD), BF16),
        scratch_shapes=[pltpu.VMEM((SSM_STATE, dm.SD), F32)],
        compiler_params=_params(("arbitrary", "arbitrary", "arbitrary"), est),
        name="ssd_scan",
    )(u, dt_pad, bias_p, alog_p, dsk_p)


def _gate_kernel(yf_ref, yb_ref, z_ref, g_ref, o_ref):
    y = yf_ref[...].astype(F32) + yb_ref[...].astype(F32)
    o_ref[...] = _rms(y * _silu(z_ref[...].astype(F32)), g_ref[...]).astype(o_ref.dtype)


def _gate_call(dm, rows, y2, big, ssm_norm):
    tm = _pick(dm.L, (256, 128))
    sd = dm.SD
    zc = 0
    est = 2 * tm * sd * 2 * 4 + 6 * tm * sd * 4
    return pl.pallas_call(
        _gate_kernel,
        grid=(rows // tm,),
        in_specs=[
            pl.BlockSpec((None, tm, sd), lambda i: (0, i, 0)),
            pl.BlockSpec((None, tm, sd), lambda i: (1, i, 0)),
            pl.BlockSpec((tm, sd), lambda i: (i, zc)),
            pl.BlockSpec((1, sd), lambda i: (0, 0)),
        ],
        out_specs=pl.BlockSpec((tm, sd), lambda i: (i, 0)),
        out_shape=jax.ShapeDtypeStruct((rows, sd), BF16),
        compiler_params=_params(("arbitrary",), est),
        name="ssd_gate_norm",
    )(y2, y2, big, ssm_norm.reshape(1, sd))


def _attn_kernel(sink_ref, q_ref, kp_ref, kc_ref, kn_ref, vp_ref, vc_ref, vn_ref, kx_ref, vx_ref, g_ref,
                 o_ref, *, KVH, G, n_lat, L):
    i = pl.program_id(1)
    is_ctx = i >= n_lat
    band = 3 * BLOCK
    width = band + L
    rowq = lax.broadcasted_iota(jnp.int32, (BLOCK, width), 0)
    colk = lax.broadcasted_iota(jnp.int32, (BLOCK, width), 1)
    first_valid = jnp.where(i == 0, BLOCK, 0)
    last_valid = jnp.where(is_ctx, -1, jnp.where(i == n_lat - 1, 2 * BLOCK - 1, band - 1))
    lower = jnp.maximum(rowq, first_valid)
    upper = jnp.minimum(rowq + 2 * WINDOW, last_valid)
    neg_inf = jnp.float32(-jnp.inf)
    in_band = jnp.where(colk >= lower, jnp.where(colk <= upper, 0.0, neg_inf), neg_inf)
    bias = jnp.where(colk >= band, 0.0, in_band)
    scale = HEAD_DIM ** -0.5
    outs = []
    for hk in range(KVH):
        ks = slice(hk * HEAD_DIM, (hk + 1) * HEAD_DIM)
        qh = jnp.concatenate([q_ref[:, (hk * G + g) * HEAD_DIM:(hk * G + g + 1) * HEAD_DIM] for g in range(G)],
                             axis=0)
        kb = jnp.concatenate([kp_ref[:, ks], kc_ref[:, ks], kn_ref[:, ks], kx_ref[:, ks]], axis=0)
        vb = jnp.concatenate([vp_ref[:, ks], vc_ref[:, ks], vn_ref[:, ks], vx_ref[:, ks]], axis=0)
        sc = lax.dot_general(qh, kb, (((1,), (1,)), ((), ())), preferred_element_type=F32) * scale
        ps, dens = [], []
        for g in range(G):
            sg = sc[g * BLOCK:(g + 1) * BLOCK, :] + bias
            sk = sink_ref[hk * G + g]
            mx = jnp.maximum(jnp.max(sg, axis=-1, keepdims=True), sk)
            p = jnp.exp(sg - mx)
            dens.append(jnp.sum(p, axis=-1, keepdims=True) + jnp.exp(sk - mx))
            ps.append(p.astype(BF16))
        pv = jnp.dot(jnp.concatenate(ps, axis=0), vb, preferred_element_type=F32)
        for g in range(G):
            outs.append(pv[g * BLOCK:(g + 1) * BLOCK, :] / dens[g])
    o = jnp.concatenate(outs, axis=1)
    o_ref[...] = _rms(o, g_ref[...]).astype(o_ref.dtype)


def _attn_call(dm, qk, big, sink, attn_norm, with_ctx):
    ns, nl = dm.S // BLOCK, dm.L // BLOCK
    lat_blocks = dm.NL // BLOCK
    steps = ns + (nl if with_ctx else 0)
    rows = dm.M if with_ctx else dm.NL
    kd, ad = dm.KD, dm.AD
    kcol, vcol = ad // kd, (dm.SD + dm.CD) // kd

    def qrow(b, i):
        return jnp.where(i < ns, b * ns + i, lat_blocks + b * nl + (i - ns))

    def band_row(b, i, off):
        return b * ns + jnp.clip(i + off, 0, ns - 1)

    def band_spec(col, off):
        return pl.BlockSpec((BLOCK, kd), lambda b, i: (band_row(b, i, off), col))

    ctx_row0 = dm.NL // dm.L
    est = 2 * BLOCK * ad * 2 * 2 + 2 * 6 * BLOCK * kd * 2 + 2 * 2 * dm.L * kd * 2 + (12 << 20)
    return pl.pallas_call(
        functools.partial(_attn_kernel, KVH=dm.KVH, G=dm.AH // dm.KVH, n_lat=ns, L=dm.L),
        grid=(dm.B, steps),
        in_specs=[
            pl.BlockSpec(memory_space=pltpu.SMEM),
            pl.BlockSpec((BLOCK, ad), lambda b, i: (qrow(b, i), 0)),
            band_spec(kcol, -1), band_spec(kcol, 0), band_spec(kcol, 1),
            band_spec(vcol, -1), band_spec(vcol, 0), band_spec(vcol, 1),
            pl.BlockSpec((dm.L, kd), lambda b, i: (ctx_row0 + b, kcol)),
            pl.BlockSpec((dm.L, kd), lambda b, i: (ctx_row0 + b, vcol)),
            pl.BlockSpec((1, ad), lambda b, i: (0, 0)),
        ],
        out_specs=pl.BlockSpec((BLOCK, ad), lambda b, i: (qrow(b, i), 0)),
        out_shape=jax.ShapeDtypeStruct((rows, ad), BF16),
        compiler_params=_params(("arbitrary", "arbitrary"), est),
        name="band_attention",
    )(sink, qk, qk, qk, qk, big, big, big, qk, big, attn_norm.reshape(1, ad))


def _router_kernel(h_ref, r_ref, o_ref, *, E):
    x = h_ref[...]
    hi = x.astype(BF16)
    lo = (x - hi.astype(F32)).astype(BF16)
    r = r_ref[...]
    a = jnp.dot(hi, r, preferred_element_type=F32)
    b = jnp.dot(lo, r, preferred_element_type=F32)
    logits = (a[:, :LANES] + a[:, LANES:]) + (b[:, :LANES] + b[:, LANES:])
    lane = lax.broadcasted_iota(jnp.int32, logits.shape, 1)
    neg_inf = jnp.float32(-jnp.inf)
    lg = jnp.where(lane < E, logits, neg_inf)
    m1 = jnp.max(lg, axis=-1, keepdims=True)
    i1 = jnp.min(jnp.where(lg == m1, lane, LANES), axis=-1, keepdims=True)
    lg2 = jnp.where(lane == i1, neg_inf, lg)
    m2 = jnp.max(lg2, axis=-1, keepdims=True)
    i2 = jnp.min(jnp.where(lg2 == m2, lane, LANES), axis=-1, keepdims=True)
    e = jnp.exp(m2 - m1)
    g1 = 1.0 / (1.0 + e)
    g2 = e / (1.0 + e)
    o_ref[...] = jnp.where(lane == 0, i1.astype(F32),
                           jnp.where(lane == 1, i2.astype(F32),
                                     jnp.where(lane == 2, g1, jnp.where(lane == 3, g2, 0.0))))


def _router_call(dm, hf, router):
    tm = _pick(dm.NL, (512, 256, 128))
    r = jnp.pad(router, ((0, 0), (0, LANES - dm.E)))
    r_hi = r.astype(BF16)
    r_lo = (r - r_hi.astype(F32)).astype(BF16)
    r2 = jnp.concatenate([r_hi, r_lo], axis=1)
    est = 2 * tm * dm.D * 4 + 2 * dm.D * 2 * LANES * 2 + 4 * tm * dm.D * 2
    return pl.pallas_call(
        functools.partial(_router_kernel, E=dm.E),
        grid=(dm.NL // tm,),
        in_specs=[pl.BlockSpec((tm, dm.D), lambda i: (i, 0)),
                  pl.BlockSpec((dm.D, 2 * LANES), lambda i: (0, 0))],
        out_specs=pl.BlockSpec((tm, LANES), lambda i: (i, 0)),
        out_shape=jax.ShapeDtypeStruct((dm.NL, LANES), F32),
        compiler_params=_params(("arbitrary",), est),
        name="moe_router",
    )(hf, r2)


def _row_copy(src_hbm, row, dst, slot, sem):
    return pltpu.make_async_copy(src_hbm.at[pl.ds(row, 1), :], dst.at[pl.ds(slot, 1), :], sem)


ROW_COPY_UNROLL = 8


def _issue_rows(src_hbm, idx_ref, dst, sem, tm):
    def issue(g, carry):
        base = g * ROW_COPY_UNROLL
        for u in range(ROW_COPY_UNROLL):
            _row_copy(src_hbm, idx_ref[0, base + u], dst, base + u, sem).start(priority=u % 2)
        return carry

    lax.fori_loop(0, tm // ROW_COPY_UNROLL, issue, 0)


def _wait_rows(src_hbm, dst, sem, tm):
    pltpu.make_async_copy(src_hbm.at[pl.ds(0, tm), :], dst, sem).wait()


def _gather_kernel(nt_ref, src_ref, src_next_ref, h_hbm, o_ref, buf, sem, *, tm):
    t = pl.program_id(0)
    nt = nt_ref[0]
    slot = t % 2

    @pl.when(t == 0)
    def _():
        _issue_rows(h_hbm, src_ref, buf.at[0], sem.at[0], tm)

    @pl.when(t + 1 < nt)
    def _():
        _issue_rows(h_hbm, src_next_ref, buf.at[1 - slot], sem.at[1 - slot], tm)

    @pl.when(t < nt)
    def _():
        _wait_rows(h_hbm, buf.at[slot], sem.at[slot], tm)
        o_ref[...] = buf[slot].astype(o_ref.dtype)

    @pl.when(t >= nt)
    def _():
        o_ref[...] = jnp.zeros_like(o_ref)


def _gather_call(dm, hf, src_tok, n_tiles, tm, nt_max):
    src3 = src_tok.reshape(nt_max, 1, tm)
    grid_spec = pltpu.PrefetchScalarGridSpec(
        num_scalar_prefetch=1, grid=(nt_max,),
        in_specs=[pl.BlockSpec((None, 1, tm), lambda t, nt: (t, 0, 0), memory_space=pltpu.SMEM),
                  pl.BlockSpec((None, 1, tm), lambda t, nt: (jnp.minimum(t + 1, nt_max - 1), 0, 0),
                               memory_space=pltpu.SMEM),
                  pl.BlockSpec(memory_space=pl.ANY)],
        out_specs=pl.BlockSpec((tm, dm.D), lambda t, nt: (t, 0)),
        scratch_shapes=[pltpu.VMEM((2, tm, dm.D), F32), pltpu.SemaphoreType.DMA((2,))])
    est = 2 * tm * dm.D * 4 + 2 * tm * dm.D * 2 + 2 * tm * dm.D * 4
    return pl.pallas_call(
        functools.partial(_gather_kernel, tm=tm), grid_spec=grid_spec,
        out_shape=jax.ShapeDtypeStruct((nt_max * tm, dm.D), BF16),
        compiler_params=_params(("arbitrary",), est),
        name="moe_gather",
    )(n_tiles, src3, src3, hf)


def _combine_kernel(p1_ref, p2_ref, p1n_ref, p2n_ref, y_hbm, g1_ref, g2_ref, x_ref, gate_ref, gpost_ref, o_ref,
                    buf, sem, *, tm, n_blocks):
    t = pl.program_id(0)
    slot = t % 2

    def issue(first_ref, second_ref, s):
        _issue_rows(y_hbm, first_ref, buf.at[s, 0], sem.at[s], tm)
        _issue_rows(y_hbm, second_ref, buf.at[s, 1], sem.at[s], tm)

    @pl.when(t == 0)
    def _():
        issue(p1_ref, p2_ref, 0)

    @pl.when(t + 1 < n_blocks)
    def _():
        issue(p1n_ref, p2n_ref, 1 - slot)

    for k in range(2):
        _wait_rows(y_hbm, buf.at[slot, k], sem.at[slot], tm)
    f = g1_ref[...] * buf[slot, 0] + g2_ref[...] * buf[slot, 1]
    o_ref[...] = x_ref[...] + gate_ref[...] * _rms(f, gpost_ref[...])


def _combine_call(dm, y_sorted, pos1, pos2, g1, g2, x, mod3, gate_lk, g_post, tm):
    d = dm.D
    nblk = dm.NL // tm
    layer, k = gate_lk
    idx_spec = pl.BlockSpec((None, 1, tm), lambda i: (i, 0, 0), memory_space=pltpu.SMEM)
    idx_next = pl.BlockSpec((None, 1, tm), lambda i: (jnp.minimum(i + 1, nblk - 1), 0, 0),
                            memory_space=pltpu.SMEM)
    col_spec = pl.BlockSpec((tm, 1), lambda i: (i, 0))
    est = 4 * tm * d * 4 + 4 * tm * d * 4 + 4 * tm * d * 4 + 4 * tm * LANES * 4
    p1, p2 = pos1.reshape(nblk, 1, tm), pos2.reshape(nblk, 1, tm)
    return pl.pallas_call(
        functools.partial(_combine_kernel, tm=tm, n_blocks=nblk),
        grid=(nblk,),
        in_specs=[idx_spec, idx_spec, idx_next, idx_next, pl.BlockSpec(memory_space=pl.ANY), col_spec, col_spec,
                  pl.BlockSpec((tm, d), lambda i: (i, 0)),
                  pl.BlockSpec((None, 1, d),
                               lambda i: ((layer * MOD_ROWS + (i * tm) // dm.S) * N_MOD + k, 0, 0)),
                  pl.BlockSpec((1, d), lambda i: (0, 0))],
        out_specs=pl.BlockSpec((tm, d), lambda i: (i, 0)),
        out_shape=jax.ShapeDtypeStruct((dm.NL, d), F32),
        scratch_shapes=[pltpu.VMEM((2, 2, tm, d), F32), pltpu.SemaphoreType.DMA((2,))],
        compiler_params=_params(("arbitrary",), est),
        name="moe_combine_residual",
    )(p1, p2, p1, p2, y_sorted, g1.reshape(dm.NL, 1), g2.reshape(dm.NL, 1), x, mod3, g_post.reshape(1, d))


def _moe_plan(dm, rout, tm):
    t = dm.NL
    e_flat = jnp.concatenate([rout[:, 0], rout[:, 1]]).astype(jnp.int32)
    onehot = (e_flat[:, None] == jnp.arange(dm.E, dtype=jnp.int32)[None, :]).astype(jnp.int32)
    before = jnp.cumsum(onehot, axis=0) - onehot
    rank = jnp.sum(before * onehot, axis=1)
    counts = jnp.sum(onehot, axis=0)
    tiles = (counts + tm - 1) // tm
    tile_end = jnp.cumsum(tiles)
    tile_start = tile_end - tiles
    dest = jnp.sum(onehot * tile_start[None, :], axis=1) * tm + rank
    nt_max = (2 * t) // tm + dm.E
    n_tiles = tile_end[-1:].astype(jnp.int32)
    tile_ids = jnp.arange(nt_max, dtype=jnp.int32)
    tile_expert = jnp.minimum(jnp.sum((tile_ids[:, None] >= tile_end[None, :]).astype(jnp.int32), axis=1),
                              dm.E - 1).astype(jnp.int32)
    ids = jnp.arange(dm.E, dtype=jnp.int32)
    later = jnp.logical_and(tiles[None, :] > 0, ids[None, :] > ids[:, None])
    next_of = jnp.min(jnp.where(later, ids[None, :], dm.E), axis=1)
    next_of = jnp.where(next_of == dm.E, -1, next_of).astype(jnp.int32)
    next_expert = jnp.sum(jnp.where(tile_expert[:, None] == ids[None, :], next_of[None, :], 0), axis=1)
    tok = jnp.concatenate([jnp.arange(t, dtype=jnp.int32)] * 2)
    src_tok = jnp.zeros((nt_max * tm,), jnp.int32).at[dest].set(tok)
    return src_tok, dest[:t], dest[t:], (tile_expert, next_expert.astype(jnp.int32), n_tiles), nt_max


def _rope_tables(dm):
    half = HEAD_DIM // 2
    freqs = ROPE_THETA ** (-jnp.arange(0, half, 2, dtype=F32) / half)
    pos = jnp.arange(dm.S, dtype=jnp.int32)
    row = (pos // dm.GW).astype(F32)[:, None] * freqs
    col = (pos % dm.GW).astype(F32)[:, None] * freqs
    cos = jnp.concatenate([jnp.cos(row), jnp.cos(row), jnp.cos(col), jnp.cos(col)], axis=1)
    sin = jnp.concatenate([-jnp.sin(row), jnp.sin(row), -jnp.sin(col), jnp.sin(col)], axis=1)
    cos = jnp.concatenate([jnp.tile(cos, (dm.B, 1)), jnp.ones((dm.NC, HEAD_DIM), F32)], axis=0)
    sin = jnp.concatenate([jnp.tile(sin, (dm.B, 1)), jnp.zeros((dm.NC, HEAD_DIM), F32)], axis=0)
    return cos, sin


def _trunk(dm, x, c, ctx, c_ctx, ada_w, ada_b, norm_mix_pre, norm_mix_post, norm_ffn_pre, norm_ffn_post,
           w_in, attn_sink, attn_norm, conv_w, conv_b, dt_bias, a_log, d_skip, ssm_norm, w_out,
           ffn_w_gate, ffn_w_up, ffn_w_down, moe_router, moe_w_gate, moe_w_up, moe_w_down):
    depth = ada_w.shape[0]
    d = dm.D
    x2 = x.reshape(dm.NL, d)
    c2 = ctx.reshape(dm.NC, d)
    cvec = jnp.concatenate([c, c_ctx[None, :], jnp.zeros((MOD_ROWS - dm.B - 1, d), F32)], axis=0)
    mod3 = _mod_call(cvec, ada_w, ada_b).reshape(depth * MOD_ROWS * N_MOD, 1, d)
    cos, sin = _rope_tables(dm)

    tm_all = _row_tile(dm.M, MM_ROW_TILE_CAP)
    tm_lat = _row_tile(dm.NL, MM_ROW_TILE_CAP)
    tn_in = _pick(math.gcd(math.gcd(dm.AD, dm.SD), math.gcd(dm.KD, dm.CD)), (512, 256, 128))
    nq, nz, nk, nx = dm.AD // tn_in, dm.SD // tn_in, dm.KD // tn_in, dm.CD // tn_in
    src_qk = lambda j: jnp.where(j < nq, j, j + nz)
    src_zxv = lambda j: jnp.where(j < nz, j + nq,
                                  jnp.where(j < nz + nx, j + nq + 2 * nk, j - nx + nq + nk))
    w_in_t = jnp.swapaxes(w_in, 1, 2)
    w_dt = w_in_t[:, dm.DT0:, :].reshape(depth, N_DIR, dm.SH, d)
    w_dt = jnp.pad(w_dt, ((0, 0), (0, 0), (0, LANES - dm.SH), (0, 0))).reshape(depth, N_DIR * LANES, d)

    xa = None
    for i in range(depth):
        last = i == depth - 1
        if i == 0:
            _, h = _norm_call(dm, dm.M, x2, c2, None, None, None, norm_mix_pre[i], (i, 0), (i, 1), mod3, BF16)
        qk = _mm_call([h], [w_in_t], i, rows=dm.M, tm=tm_all, tn=tn_in, n_cols=dm.AD + dm.KD, out_dtype=BF16,
                      rope=(cos, sin), w_nt=True, panel_src=src_qk, name="proj_in_qk")
        big = _mm_call([h], [w_in_t], i, rows=dm.M, tm=tm_all, tn=tn_in, n_cols=dm.SD + dm.CD + dm.KD,
                       out_dtype=BF16, w_nt=True, panel_src=src_zxv, name="proj_in_zxv")
        dt_pad = _mm_call([h], [w_dt], i, rows=dm.M, tm=tm_all, tn=N_DIR * LANES, n_cols=N_DIR * LANES,
                          out_dtype=F32, w_nt=True, name="proj_dt")
        u = _conv_call(dm, big, conv_w[i], conv_b[i])
        y2 = _ssd_call(dm, u, dt_pad, dt_bias[i], a_log[i], d_skip[i])
        rows = dm.NL if last else dm.M
        tm_rows = tm_lat if last else tm_all
        ssm = _gate_call(dm, rows, y2, big, ssm_norm[i])
        attn = _attn_call(dm, qk, big, attn_sink[i], attn_norm[i], with_ctx=not last)
        y = _mm_call([attn, ssm], [w_out], i, rows=rows, tm=tm_rows, tn=_pick(d, (512, 256, 128)), n_cols=d,
                     out_dtype=BF16, name="proj_out")
        if i == 0:
            xa, hf = _norm_call(dm, rows, x2, c2, y, norm_mix_post[i], (i, 2), norm_ffn_pre[i], (i, 3), (i, 4),
                                mod3, BF16 if i % 2 == 0 else F32)
        else:
            xa, hf = _norm_call(dm, rows, xa, None, y, norm_mix_post[i], (i, 2), norm_ffn_pre[i], (i, 3), (i, 4),
                                mod3, BF16 if i % 2 == 0 else F32)
        j = i // 2
        if i % 2 == 0:
            act = _mm_call([hf], [ffn_w_gate, ffn_w_up], j, rows=rows, tm=tm_rows,
                           tn=_pick(dm.DFF, (256, 128)), n_cols=dm.DFF, out_dtype=BF16, mode="swiglu",
                           name="ffn_gate_up")
            f = _mm_call([act], [ffn_w_down], j, rows=rows, tm=_row_tile(rows, MM_ROW_TILE_CAP // 4),
                         tn=_pick(d, (512, 256, 128)), n_cols=d, out_dtype=BF16, name="ffn_down")
            if last:
                raise NotImplementedError("dense FFN on the last layer")
            xa, h = _norm_call(dm, rows, xa, None, f, norm_ffn_post[i], (i, 5), norm_mix_pre[i + 1],
                               (i + 1, 0), (i + 1, 1), mod3, BF16)
        else:
            if not last:
                raise NotImplementedError("MoE FFN on a non-final layer")
            tm_e = _pick(dm.NL, (256, 128))
            rout = _router_call(dm, hf, moe_router[j])
            src_tok, pos1, pos2, tables, nt_max = _moe_plan(dm, rout, tm_e)
            xs = _gather_call(dm, hf, src_tok, tables[2], tm_e, nt_max)
            n_exp = moe_w_gate.shape[1]
            wg = moe_w_gate.reshape((-1,) + moe_w_gate.shape[2:])
            wu = moe_w_up.reshape((-1,) + moe_w_up.shape[2:])
            wd = moe_w_down.reshape((-1,) + moe_w_down.shape[2:])
            act = _mm_call([xs], [wg, wu], j * n_exp, rows=nt_max * tm_e, tm=tm_e,
                           tn=_pick(dm.DE, (512, 256, 128)), n_cols=dm.DE, out_dtype=BF16, mode="swiglu",
                           grouped=tables, name="moe_gate_up")
            ys = _mm_call([act], [wd], j * n_exp, rows=nt_max * tm_e, tm=tm_e,
                          tn=_pick(d, (1024, 512, 256, 128)), n_cols=d, out_dtype=F32,
                          grouped=tables, name="moe_down")
            xa = _combine_call(dm, ys, pos1, pos2, rout[:, 2], rout[:, 3], xa, mod3, (i, 5), norm_ffn_post[i],
                               tm_e)
    return xa[:dm.NL].reshape(dm.B, dm.S, d)


def kernel(x, c, ctx, c_ctx, ada_w, ada_b, norm_mix_pre, norm_mix_post, norm_ffn_pre, norm_ffn_post, w_in,
           attn_sink, attn_norm, conv_w, conv_b, dt_bias, a_log, d_skip, ssm_norm, w_out, ffn_w_gate, ffn_w_up,
           ffn_w_down, moe_router, moe_w_gate, moe_w_up, moe_w_down):
    b, s, d = x.shape
    sh = a_log.shape[2]
    cd = conv_w.shape[2]
    sd = sh * SSM_HEAD_DIM
    dm = Dims(B=b, S=s, L=ctx.shape[1], D=d, AH=attn_sink.shape[1], KVH=ATTN_KV_HEADS, SH=sh,
              SG=(cd - sd) // (2 * SSM_STATE), DFF=ffn_w_gate.shape[2], E=moe_router.shape[2],
              DE=moe_w_gate.shape[3], GW=GRID_W)
    assert w_in.shape[2] == dm.DT0 + N_DIR * dm.SH and w_out.shape[1] == dm.AD + dm.SD and dm.AD == dm.SD
    assert dm.S % BLOCK == 0 and dm.L % BLOCK == 0 and dm.SH <= LANES and dm.E <= LANES
    return _trunk(dm, x, c, ctx, c_ctx, ada_w, ada_b, norm_mix_pre, norm_mix_post, norm_ffn_pre, norm_ffn_post,
                  w_in, attn_sink, attn_norm, conv_w, conv_b, dt_bias, a_log, d_skip, ssm_norm, w_out,
                  ffn_w_gate, ffn_w_up, ffn_w_down, moe_router, moe_w_gate, moe_w_up, moe_w_down)
```
